```python
import jax, jax.numpy as jnp
from jax import lax
import numpy as np

D_MODEL = 2048
BATCH = 1
SEQ = 8192
DEPTH = 2
DEC_BATCH = 32
DEC_SEQ = 8
PAST_LEN = 8192
PAGE_SIZE = 128

N_A_LAYERS = DEPTH // 2
N_B_LAYERS = DEPTH - N_A_LAYERS
SSM_EXPAND = 2
D_INNER = SSM_EXPAND * D_MODEL
SSM_HEAD_DIM = 64
SSM_HEADS = D_INNER // SSM_HEAD_DIM
SSM_GROUPS = 8
SSM_STATE = 128
CONV_WIDTH = 4
CONV_DIM = D_INNER + 2 * SSM_GROUPS * SSM_STATE
D_IN_PROJ = D_INNER + CONV_DIM + SSM_HEADS
SSD_CHUNK = 128
ATT_HEAD_DIM = 128
ATT_HEADS = D_MODEL // ATT_HEAD_DIM
Q_BLOCK = 128
FORGET_BIAS_INIT = 4.0
N_EXPERT_GROUPS = 4
EXPERTS_PER_GROUP = 4
N_EXPERTS = N_EXPERT_GROUPS * EXPERTS_PER_GROUP
TOP_K_IN_GROUP = 2
D_EXPERT = D_MODEL // 4
RMS_EPS = 1e-5

kernel_name = 'yoco_mamba2_fox_hmoe_step'


def rms_norm(x, g):
    xf = x.astype(jnp.float32)
    xf = xf * lax.rsqrt(jnp.mean(xf * xf, axis=-1, keepdims=True) + RMS_EPS)
    return (xf * g.astype(jnp.float32)).astype(x.dtype)


def grouped_rms_norm(y, g, n_groups):
    shp = y.shape
    yf = y.astype(jnp.float32).reshape(shp[:-1] + (n_groups, shp[-1] // n_groups))
    yf = yf * lax.rsqrt(jnp.mean(yf * yf, axis=-1, keepdims=True) + RMS_EPS)
    return (yf.reshape(shp) * g.astype(jnp.float32)).astype(y.dtype)


def ssd_chunked(xh, da, bm, cm, h0, chunk):
    b, L, H, P = xh.shape
    G, N = bm.shape[2], bm.shape[3]
    R = H // G
    nc = -(-L // chunk)
    pad = nc * chunk - L

    def prep(t):
        t = t.astype(jnp.float32)
        return jnp.pad(t, [(0, 0), (0, pad)] + [(0, 0)] * (t.ndim - 2))

    x = prep(xh).reshape(b, nc, chunk, G, R, P)
    a = prep(da).reshape(b, nc, chunk, G, R)
    bc = prep(bm).reshape(b, nc, chunk, G, N)
    cc = prep(cm).reshape(b, nc, chunk, G, N)
    a_cum = jnp.cumsum(a, axis=2)
    causal = jnp.tril(jnp.ones((chunk, chunk), bool))[:, :, None, None]
    seg = a_cum[:, :, :, None] - a_cum[:, :, None, :]
    decay_ls = jnp.exp(jnp.where(causal, seg, -jnp.inf))
    cb = jnp.einsum('bclgn,bcsgn->bclsg', cc, bc)
    y_diag = jnp.einsum('bclsgr,bcsgrp->bclgrp', cb[..., None] * decay_ls, x)
    decay_to_end = jnp.exp(a_cum[:, :, -1:] - a_cum)
    chunk_states = jnp.einsum('bclgn,bclgr,bclgrp->bcgrpn', bc, decay_to_end, x)
    chunk_decay = jnp.exp(a_cum[:, :, -1])

    def step(h, inp):
        s_c, d_c = inp
        return h * d_c[..., None, None] + s_c, h

    h_last, h_enter = lax.scan(step, h0.astype(jnp.float32).reshape(b, G, R, P, N),
                               (jnp.moveaxis(chunk_states, 1, 0), jnp.moveaxis(chunk_decay, 1, 0)))
    h_enter = jnp.moveaxis(h_enter, 0, 1)
    y_off = jnp.einsum('bclgn,bcgrpn,bclgr->bclgrp', cc, h_enter, jnp.exp(a_cum))
    y = (y_diag + y_off).reshape(b, nc * chunk, H, P)[:, :L]
    return y.astype(xh.dtype), h_last.reshape(b, H, P, N).astype(h0.dtype)


def mamba2_mixer(h, conv_state, ssm_state, p, i):
    b, L, _ = h.shape
    proj = jnp.einsum('bld,de->ble', h, p['a_w_in'][i])
    z = proj[..., :D_INNER]
    xbc = proj[..., D_INNER:D_INNER + CONV_DIM]
    dt_raw = proj[..., D_INNER + CONV_DIM:]
    xbc_full = jnp.concatenate([conv_state.astype(xbc.dtype), xbc], axis=1)
    new_conv = xbc_full[:, L:]
    conv_w = p['a_conv_w'][i]
    acc = p['a_conv_b'][i]
    for tap in range(CONV_WIDTH):
        acc = acc + xbc_full[:, tap:tap + L] * conv_w[tap]
    xbc = jax.nn.silu(acc)
    gn = SSM_GROUPS * SSM_STATE
    xs = xbc[..., :D_INNER].reshape(b, L, SSM_HEADS, SSM_HEAD_DIM)
    bm = xbc[..., D_INNER:D_INNER + gn].reshape(b, L, SSM_GROUPS, SSM_STATE)
    cm = xbc[..., D_INNER + gn:].reshape(b, L, SSM_GROUPS, SSM_STATE)
    dt = jax.nn.softplus((dt_raw + p['a_dt_bias'][i]).astype(jnp.float32))
    a = -jnp.exp(p['a_a_log'][i].astype(jnp.float32))
    y, h_last = ssd_chunked(xs * dt[..., None].astype(xs.dtype), dt * a, bm, cm, ssm_state,
                            min(SSD_CHUNK, L))
    y = y + xs * p['a_d_skip'][i][:, None]
    y = y.reshape(b, L, D_INNER) * jax.nn.silu(z)
    y = grouped_rms_norm(y, p['a_norm'][i], SSM_GROUPS)
    return jnp.einsum('ble,ed->bld', y, p['a_w_out'][i]), new_conv, h_last


def shared_kv(x, p):
    b, L, _ = x.shape
    hd = ATT_HEADS * ATT_HEAD_DIM
    proj = jnp.einsum('bld,de->ble', rms_norm(x, p['kv_norm']), p['w_kv'])
    k = rms_norm(proj[..., :hd].reshape(b, L, ATT_HEADS, ATT_HEAD_DIM), p['k_norm'])
    v = proj[..., hd:2 * hd].reshape(b, L, ATT_HEADS, ATT_HEAD_DIM)
    logf = jax.nn.log_sigmoid((proj[..., 2 * hd:] + p['b_forget']).astype(jnp.float32))
    return jnp.stack([k, v], axis=2), logf


def fox_attend_prompt(q, k, v, logf):
    b, L, H, Dh = q.shape
    scale = Dh ** -0.5
    c_k = jnp.moveaxis(jnp.cumsum(logf.astype(jnp.float32), axis=1), 1, 2)
    nb = L // Q_BLOCK
    q_blocks = jnp.moveaxis(q.reshape(b, nb, Q_BLOCK, H, Dh), 1, 0)
    c_blocks = jnp.moveaxis(c_k.reshape(b, H, nb, Q_BLOCK), 2, 0)
    starts = jnp.arange(nb) * Q_BLOCK
    key_pos = jnp.arange(L)

    def block(args):
        qi, ci, start = args
        s = jnp.einsum('bqhd,bkhd->bhqk', qi, k, preferred_element_type=jnp.float32) * scale
        s = s + ci[..., None] - c_k[:, :, None, :]
        q_pos = start + jnp.arange(Q_BLOCK)
        s = jnp.where(key_pos[None, :] <= q_pos[:, None], s, -jnp.inf)
        pr = jax.nn.softmax(s, axis=-1)
        return jnp.einsum('bhqk,bkhd->bqhd', pr.astype(v.dtype), v)

    out = lax.map(block, (q_blocks, c_blocks, starts))
    return jnp.moveaxis(out, 0, 1).reshape(b, L, H, Dh)


def online_update(carry, s, v):
    m, l, acc = carry
    m_new = jnp.maximum(m, jnp.max(s, axis=-1))
    alpha = jnp.exp(m - m_new)
    pr = jnp.exp(s - m_new[..., None])
    return (m_new, l * alpha + jnp.sum(pr, axis=-1),
            acc * alpha[..., None] + jnp.einsum('bhqk,bkhd->bhqd', pr, v.astype(jnp.float32)))


def fox_attend_sample(q, k, v, logf, cache_kv, cache_logf, page_table):
    b, T, H, Dh = q.shape
    scale = Dh ** -0.5
    n_pages = page_table.shape[1]
    page = cache_logf.shape[1]
    logf_past = cache_logf[page_table].reshape(b, n_pages * page, H).astype(jnp.float32)
    d_past = lax.cumsum(logf_past, axis=1, reverse=True) - logf_past
    c_q = jnp.moveaxis(jnp.cumsum(logf.astype(jnp.float32), axis=1), 1, 2)
    d_pages = jnp.moveaxis(d_past.reshape(b, n_pages, page, H), 1, 0)

    def page_step(carry, inp):
        pt, d_p = inp
        kv = cache_kv[pt]
        s = jnp.einsum('bqhd,bkhd->bhqk', q, kv[:, :, 0], preferred_element_type=jnp.float32) * scale
        s = s + c_q[..., None] + jnp.moveaxis(d_p, 1, 2)[:, :, None, :]
        return online_update(carry, s, kv[:, :, 1]), None

    init = (jnp.full((b, H, T), -jnp.inf, jnp.float32), jnp.zeros((b, H, T), jnp.float32),
            jnp.zeros((b, H, T, Dh), jnp.float32))
    carry, _ = lax.scan(page_step, init, (page_table.T, d_pages))
    s_new = jnp.einsum('bqhd,bkhd->bhqk', q, k, preferred_element_type=jnp.float32) * scale
    s_new = s_new + c_q[..., :, None] - c_q[..., None, :]
    s_new = jnp.where(jnp.tril(jnp.ones((T, T), bool)), s_new, -jnp.inf)
    _, l, acc = online_update(carry, s_new, v)
    return jnp.moveaxis(acc / l[..., None], 1, 2).astype(q.dtype)


def fox_mixer(h, kv, logf, attend, p, j):
    b, L, _ = h.shape
    q = jnp.einsum('bld,de->ble', h, p['b_w_q'][j]).reshape(b, L, ATT_HEADS, ATT_HEAD_DIM)
    q = rms_norm(q, p['b_q_norm'][j])
    o = attend(q, kv[:, :, 0], kv[:, :, 1], logf)
    return jnp.einsum('ble,ed->bld', o.reshape(b, L, ATT_HEADS * ATT_HEAD_DIM), p['b_w_o'][j])


def hier_moe(h, p, i):
    g_logits = jnp.einsum('bld,dg->blg', h, p['m_w_group'][i], preferred_element_type=jnp.float32) \
        + p['m_b_group'][i]
    g_prob = jax.nn.softmax(g_logits, axis=-1)
    g_idx = jnp.argmax(g_logits, axis=-1)
    g_w = jnp.take_along_axis(g_prob, g_idx[..., None], axis=-1)
    e_logits = jnp.einsum('bld,dge->blge', h, p['m_w_expert'][i], preferred_element_type=jnp.float32) \
        + p['m_b_expert'][i]
    e_sel = jnp.take_along_axis(e_logits, g_idx[..., None, None], axis=2)[:, :, 0]
    top_v, top_i = lax.top_k(e_sel, TOP_K_IN_GROUP)
    w_top = jax.nn.softmax(top_v, axis=-1) * g_w
    flat = g_idx[..., None] * EXPERTS_PER_GROUP + top_i
    gates = jnp.sum(jax.nn.one_hot(flat, N_EXPERTS, dtype=jnp.float32) * w_top[..., None], axis=-2)
    hidden = jax.nn.silu(jnp.einsum('bld,edf->blef', h, p['m_w_gate'][i])) \
        * jnp.einsum('bld,edf->blef', h, p['m_w_up'][i])
    hidden = hidden * gates[..., None].astype(hidden.dtype)
    return jnp.einsum('blef,efd->bld', hidden, p['m_w_down'][i])


def trunk(x, conv_init, ssm_init, attend, p):
    ssm_new, conv_new = [], []
    kv = logf = None
    for layer in range(DEPTH):
        h = rms_norm(x, p['norm_mix'][layer])
        if layer < N_A_LAYERS:
            y, conv_s, ssm_s = mamba2_mixer(h, conv_init[layer], ssm_init[layer], p, layer)
            conv_new.append(conv_s)
            ssm_new.append(ssm_s)
        else:
            y = fox_mixer(h, kv, logf, attend, p, layer - N_A_LAYERS)
        x = x + y
        x = x + hier_moe(rms_norm(x, p['norm_ffn'][layer]), p, layer)
        if layer == N_A_LAYERS - 1:
            kv, logf = shared_kv(x, p)
    return x, jnp.stack(ssm_new), jnp.stack(conv_new), kv, logf


def setup_inputs(seed: int = 0) -> dict:
    key = jax.random.key(seed)
    it = iter(jax.random.split(key, 40))

    def nrm(shape, scale):
        return jax.random.normal(next(it), shape, jnp.float32) * scale

    n_pages = PAST_LEN // PAGE_SIZE
    n_used = DEC_BATCH * n_pages
    n_pool = n_used + (n_used + 3) // 4
    hd = ATT_HEADS * ATT_HEAD_DIM
    x_prompt = nrm((BATCH, SEQ, D_MODEL), 1.0)
    x_sample = nrm((DEC_BATCH, DEC_SEQ, D_MODEL), 1.0)
    state_ssm = nrm((N_A_LAYERS, DEC_BATCH, SSM_HEADS, SSM_HEAD_DIM, SSM_STATE), 0.1)
    state_conv = nrm((N_A_LAYERS, DEC_BATCH, CONV_WIDTH - 1, CONV_DIM), 1.0)
    cache_kv = nrm((n_pool, PAGE_SIZE, 2, ATT_HEADS, ATT_HEAD_DIM), 1.0)
    cache_logf = jax.nn.log_sigmoid(FORGET_BIAS_INIT + nrm((n_pool, PAGE_SIZE, ATT_HEADS), 1.0))
    page_table = jax.random.permutation(next(it), n_pool)[:n_used].reshape(DEC_BATCH, n_pages).astype(jnp.int32)
    dt0 = jnp.exp(jax.random.uniform(next(it), (N_A_LAYERS, SSM_HEADS), jnp.float32,
                                     minval=np.log(1e-3), maxval=np.log(1e-1)))
    a_dt_bias = dt0 + jnp.log(-jnp.expm1(-dt0))
    a_a_log = jnp.log(jax.random.uniform(next(it), (N_A_LAYERS, SSM_HEADS), jnp.float32, minval=1.0, maxval=16.0))
    return {
        'x_prompt': x_prompt,
        'x_sample': x_sample,
        'state_ssm': state_ssm,
        'state_conv': state_conv,
        'cache_kv': cache_kv,
        'cache_logf': cache_logf,
        'page_table': page_table,
        'norm_mix': 1.0 + nrm((DEPTH, D_MODEL), 0.02),
        'norm_ffn': 1.0 + nrm((DEPTH, D_MODEL), 0.02),
        'a_w_in': nrm((N_A_LAYERS, D_MODEL, D_IN_PROJ), D_MODEL ** -0.5),
        'a_conv_w': nrm((N_A_LAYERS, CONV_WIDTH, CONV_DIM), CONV_WIDTH ** -0.5),
        'a_conv_b': nrm((N_A_LAYERS, CONV_DIM), 0.02),
        'a_dt_bias': a_dt_bias,
        'a_a_log': a_a_log,
        'a_d_skip': 1.0 + nrm((N_A_LAYERS, SSM_HEADS), 0.02),
        'a_norm': 1.0 + nrm((N_A_LAYERS, D_INNER), 0.02),
        'a_w_out': nrm((N_A_LAYERS, D_INNER, D_MODEL), D_INNER ** -0.5),
        'kv_norm': 1.0 + nrm((D_MODEL,), 0.02),
        'w_kv': nrm((D_MODEL, 2 * hd + ATT_HEADS), D_MODEL ** -0.5),
        'b_forget': FORGET_BIAS_INIT + nrm((ATT_HEADS,), 0.1),
        'k_norm': 1.0 + nrm((ATT_HEAD_DIM,), 0.02),
        'b_w_q': nrm((N_B_LAYERS, D_MODEL, hd), D_MODEL ** -0.5),
        'b_q_norm': 1.0 + nrm((N_B_LAYERS, ATT_HEAD_DIM), 0.02),
        'b_w_o': nrm((N_B_LAYERS, hd, D_MODEL), hd ** -0.5),
        'm_w_group': nrm((DEPTH, D_MODEL, N_EXPERT_GROUPS), D_MODEL ** -0.5),
        'm_b_group': nrm((DEPTH, N_EXPERT_GROUPS), 0.01),
        'm_w_expert': nrm((DEPTH, D_MODEL, N_EXPERT_GROUPS, EXPERTS_PER_GROUP), D_MODEL ** -0.5),
        'm_b_expert': nrm((DEPTH, N_EXPERT_GROUPS, EXPERTS_PER_GROUP), 0.01),
        'm_w_gate': nrm((DEPTH, N_EXPERTS, D_MODEL, D_EXPERT), D_MODEL ** -0.5),
        'm_w_up': nrm((DEPTH, N_EXPERTS, D_MODEL, D_EXPERT), D_MODEL ** -0.5),
        'm_w_down': nrm((DEPTH, N_EXPERTS, D_EXPERT, D_MODEL), D_EXPERT ** -0.5),
    }


def reference(x_prompt, x_sample, state_ssm, state_conv, cache_kv, cache_logf, page_table,
              norm_mix, norm_ffn, a_w_in, a_conv_w, a_conv_b, a_dt_bias, a_a_log, a_d_skip, a_norm,
              a_w_out, kv_norm, w_kv, b_forget, k_norm, b_w_q, b_q_norm, b_w_o, m_w_group, m_b_group,
              m_w_expert, m_b_expert, m_w_gate, m_w_up, m_w_down):
    p = {'norm_mix': norm_mix, 'norm_ffn': norm_ffn, 'a_w_in': a_w_in, 'a_conv_w': a_conv_w,
         'a_conv_b': a_conv_b, 'a_dt_bias': a_dt_bias, 'a_a_log': a_a_log, 'a_d_skip': a_d_skip,
         'a_norm': a_norm, 'a_w_out': a_w_out, 'kv_norm': kv_norm, 'w_kv': w_kv, 'b_forget': b_forget,
         'k_norm': k_norm, 'b_w_q': b_w_q, 'b_q_norm': b_q_norm, 'b_w_o': b_w_o,
         'm_w_group': m_w_group, 'm_b_group': m_b_group, 'm_w_expert': m_w_expert,
         'm_b_expert': m_b_expert, 'm_w_gate': m_w_gate, 'm_w_up': m_w_up, 'm_w_down': m_w_down}
    b_p = x_prompt.shape[0]
    conv0 = jnp.zeros((N_A_LAYERS, b_p, CONV_WIDTH - 1, CONV_DIM), x_prompt.dtype)
    ssm0 = jnp.zeros((N_A_LAYERS, b_p, SSM_HEADS, SSM_HEAD_DIM, SSM_STATE), x_prompt.dtype)
    y_prompt, ssm_prompt, conv_prompt, kv_prompt, logf_prompt = trunk(
        x_prompt, conv0, ssm0, fox_attend_prompt, p)

    def sample_attend(q, k, v, logf):
        return fox_attend_sample(q, k, v, logf, cache_kv, cache_logf, page_table)

    y_sample, ssm_sample, conv_sample, kv_sample, logf_sample = trunk(
        x_sample, state_conv, state_ssm, sample_attend, p)
    return (y_prompt, y_sample, ssm_prompt, conv_prompt, kv_prompt, logf_prompt,
            ssm_sample, conv_sample, kv_sample, logf_sample)
```

```python
import functools

import jax
import jax.numpy as jnp
from jax import lax
from jax.experimental import pallas as pl
from jax.experimental.pallas import tpu as pltpu

F32 = jnp.float32
BF16 = jnp.bfloat16

LANES = 128
SUBLANES = 8
VMEM_LIMIT = 56 * 1024 * 1024

RMS_EPS = 1e-5
SSM_HEAD_DIM = 64
SSM_GROUPS = 8
SSM_STATE = 128
CONV_WIDTH = 4
ATT_HEAD_DIM = 128
N_EXPERT_GROUPS = 4
EXPERTS_PER_GROUP = 4
N_EXPERTS = N_EXPERT_GROUPS * EXPERTS_PER_GROUP


def _params(*sem):
    return pltpu.CompilerParams(dimension_semantics=sem, vmem_limit_bytes=VMEM_LIMIT)


def _dot(a, b):
    return jnp.dot(a, b, preferred_element_type=F32)


def _dot_nt(a, b):
    return lax.dot_general(a, b, (((1,), (1,)), ((), ())), preferred_element_type=F32)


def _dot_tn(a, b):
    return lax.dot_general(a, b, (((0,), (0,)), ((), ())), preferred_element_type=F32)


def _split_bf16(v, terms):
    out = []
    r = v
    for _ in range(terms):
        t = r.astype(BF16)
        out.append(t)
        r = r - t.astype(F32)
    return out


def _silu(x):
    return x / (1.0 + jnp.exp(-x))


def _softplus(x):
    return jnp.maximum(x, 0.0) + jnp.log1p(jnp.exp(-jnp.abs(x)))


def _rms_rows(x, g):
    ms = jnp.mean(x * x, axis=-1, keepdims=True)
    return x * lax.rsqrt(ms + RMS_EPS) * g


def _norm_matmul_kernel(*refs, n_normed_tiles, has_head_norm, out_scale):
    if has_head_norm:
        x_ref, g_ref, w_ref, hg_ref, o_ref, h_scr = refs
    else:
        x_ref, g_ref, w_ref, o_ref, h_scr = refs
        hg_ref = None
    j = pl.program_id(1)

    @pl.when(j == 0)
    def _():
        h_scr[...] = _rms_rows(x_ref[...], g_ref[...]).astype(BF16)

    acc = _dot(h_scr[...], w_ref[...])
    tn = acc.shape[1]

    def head_normed(a):
        parts = []
        for c in range(tn // ATT_HEAD_DIM):
            blk = a[:, c * ATT_HEAD_DIM:(c + 1) * ATT_HEAD_DIM]
            parts.append(_rms_rows(blk, hg_ref[...]))
        return jnp.concatenate(parts, axis=1)

    def emit(a):
        if out_scale != 1.0:
            a = a * out_scale
        o_ref[...] = a.astype(o_ref.dtype)

    if not has_head_norm:
        emit(acc)
    else:
        @pl.when(j < n_normed_tiles)
        def _():
            emit(head_normed(acc))

        @pl.when(j >= n_normed_tiles)
        def _():
            emit(acc)


def norm_matmul(x, g, w, *, tm, tn, head_g=None, n_normed_cols=0, out_scale=1.0, out_dtype=F32,
                name="norm_matmul"):
    m, k = x.shape
    n = w.shape[1]
    assert m % tm == 0 and n % tn == 0 and n_normed_cols % tn == 0
    has_hn = head_g is not None
    in_specs = [pl.BlockSpec((tm, k), lambda i, j: (i, 0)),
                pl.BlockSpec((1, k), lambda i, j: (0, 0)),
                pl.BlockSpec((k, tn), lambda i, j: (0, j))]
    args = [x, g.reshape(1, k), w]
    if has_hn:
        in_specs.append(pl.BlockSpec((1, ATT_HEAD_DIM), lambda i, j: (0, 0)))
        args.append(head_g.reshape(1, ATT_HEAD_DIM))
    return pl.pallas_call(
        functools.partial(_norm_matmul_kernel, n_normed_tiles=n_normed_cols // tn,
                          has_head_norm=has_hn, out_scale=out_scale),
        grid=(m // tm, n // tn),
        in_specs=in_specs,
        out_specs=pl.BlockSpec((tm, tn), lambda i, j: (i, j)),
        out_shape=jax.ShapeDtypeStruct((m, n), out_dtype),
        scratch_shapes=[pltpu.VMEM((tm, k), BF16)],
        compiler_params=_params("parallel", "arbitrary"),
        name=name,
    )(*args)


def _matmul_res_kernel(a_ref, w_ref, r_ref, o_ref):
    o_ref[...] = r_ref[...] + _dot(a_ref[...].astype(BF16), w_ref[...])


def matmul_res(a, w, res, *, tm, tn, name="matmul_res"):
    m, k = a.shape
    n = w.shape[1]
    assert m % tm == 0 and n % tn == 0
    return pl.pallas_call(
        _matmul_res_kernel,
        grid=(m // tm, n // tn),
        in_specs=[pl.BlockSpec((tm, k), lambda i, j: (i, 0)),
                  pl.BlockSpec((k, tn), lambda i, j: (0, j)),
                  pl.BlockSpec((tm, tn), lambda i, j: (i, j))],
        out_specs=pl.BlockSpec((tm, tn), lambda i, j: (i, j)),
        out_shape=jax.ShapeDtypeStruct((m, n), F32),
        compiler_params=_params("parallel", "arbitrary"),
        name=name,
    )(a, w, res)


def _ssd_kernel(z_ref, xs_ref, b_ref, c_ref, dt_ref, conv0_ref, ssm0_ref, cw_ref, cbias_ref,
                dtb_ref, alog_ref, dskip_ref, gn_ref, expand_ref, tril_ref, eye_ref,
                yn_ref, ssm_out_ref, conv_out_ref,
                ext_scr, act_scr, y_scr, h_scr, *, chunk, n_heads, d_inner, gn_cols):
    q = chunk
    c = pl.program_id(1)
    nc = pl.num_programs(1)
    pad = SUBLANES
    heads_per_group = n_heads // SSM_GROUPS
    gcols = heads_per_group * SSM_HEAD_DIM

    @pl.when(c == 0)
    def _():
        ext_scr[0:pad, :] = conv0_ref[0]
        h_scr[...] = ssm0_ref[0]

    ext_scr[pad:pad + q, 0:d_inner] = xs_ref[...]
    ext_scr[pad:pad + q, d_inner:d_inner + gn_cols] = b_ref[...]
    ext_scr[pad:pad + q, d_inner + gn_cols:] = c_ref[...]

    acc = jnp.broadcast_to(cbias_ref[...], (q, ext_scr.shape[1]))
    for tap in range(CONV_WIDTH):
        start = pad - (CONV_WIDTH - 1) + tap
        acc = acc + ext_scr[start:start + q, :] * cw_ref[tap:tap + 1, :]
    act_scr[...] = _silu(acc)

    tail = ext_scr[q:q + pad, :]
    conv_out_ref[0] = tail
    ext_scr[0:pad, :] = tail

    dt = _softplus(dt_ref[...] + dtb_ref[...])
    a = dt * (-jnp.exp(alog_ref[...]))
    tril = tril_ref[...]
    eye = eye_ref[...]
    a_cum = sum(_dot(tril, t) for t in _split_bf16(a, 3))
    a_cum_t = sum(_dot_nt(eye, t) for t in _split_bf16(a_cum, 3))
    dt_t = sum(_dot_nt(eye, t) for t in _split_bf16(dt, 3))
    a_last = a_cum[q - 1:q, :]
    w_state = dt * jnp.exp(a_last - a_cum)
    e_cum = jnp.exp(a_cum)
    expand = expand_ref[...]
    w_state_x = sum(_dot(t, expand) for t in _split_bf16(w_state, 2))
    e_cum_x = sum(_dot(t, expand) for t in _split_bf16(e_cum, 2))

    row_i = lax.broadcasted_iota(jnp.int32, (q, q), 0)
    col_i = lax.broadcasted_iota(jnp.int32, (q, q), 1)
    causal = col_i <= row_i
    lane = lax.broadcasted_iota(jnp.int32, (q, LANES), 1)
    lo_half = lane < SSM_HEAD_DIM

    for g in range(SSM_GROUPS):
        bg = act_scr[:, d_inner + g * SSM_STATE:d_inner + (g + 1) * SSM_STATE].astype(BF16)
        cg = act_scr[:, d_inner + gn_cols + g * SSM_STATE:
                     d_inner + gn_cols + (g + 1) * SSM_STATE].astype(BF16)
        cb = _dot_nt(cg, bg)
        for j in range(gcols // LANES):
            c0 = g * gcols + j * LANES
            xs_blk = act_scr[:, c0:c0 + LANES]
            y_pair = None
            for half in range(2):
                h = (c0 // SSM_HEAD_DIM) + half
                seg = a_cum[:, h:h + 1] - a_cum_t[h:h + 1, :]
                m = cb * jnp.exp(jnp.where(causal, seg, -jnp.inf)) * dt_t[h:h + 1, :]
                keep = lo_half if half == 0 else jnp.logical_not(lo_half)
                xh = jnp.where(keep, xs_blk, 0.0).astype(BF16)
                part = _dot(m.astype(BF16), xh)
                y_pair = part if y_pair is None else y_pair + part
            y_scr[:, c0:c0 + LANES] = y_pair
        gs = slice(g * gcols, (g + 1) * gcols)
        h_g = h_scr[:, gs]
        y_off = _dot(cg, h_g.astype(BF16)) * e_cum_x[:, gs]
        y_scr[:, gs] = y_scr[:, gs] + y_off
        xw = (act_scr[:, gs] * w_state_x[:, gs]).astype(BF16)
        h_scr[:, gs] = h_g * e_cum_x[q - 1:q, gs] + _dot_tn(bg, xw)

    y = y_scr[...] + act_scr[:, 0:d_inner] * dskip_ref[...]
    y = y * _silu(z_ref[...])
    gw = d_inner // SSM_GROUPS
    parts = []
    for g in range(SSM_GROUPS):
        parts.append(_rms_rows(y[:, g * gw:(g + 1) * gw], gn_ref[:, g * gw:(g + 1) * gw]))
    yn_ref[...] = jnp.concatenate(parts, axis=1).astype(yn_ref.dtype)

    @pl.when(c == nc - 1)
    def _():
        ssm_out_ref[0] = h_scr[...]


def ssd_mixer(proj, conv_state, ssm_state, conv_w, conv_b, dt_bias, a_log, d_skip, gn,
              *, batch, seq, chunk, d_inner, name="ssd"):
    n_heads = d_inner // SSM_HEAD_DIM
    gn_cols = SSM_GROUPS * SSM_STATE
    conv_dim = d_inner + 2 * gn_cols
    nc = seq // chunk
    assert seq % chunk == 0 and n_heads <= LANES
    q = chunk
    pad = SUBLANES

    conv0 = jnp.concatenate(
        [jnp.zeros((batch, pad - (CONV_WIDTH - 1), conv_dim), F32), conv_state], axis=1)
    ssm0 = jnp.transpose(ssm_state.reshape(batch, d_inner, SSM_STATE), (0, 2, 1))

    def lane_pad(v):
        return jnp.pad(v, (0, LANES - v.shape[0])).reshape(1, LANES)

    head_of_col = jnp.arange(d_inner) // SSM_HEAD_DIM
    expand = (jnp.arange(LANES)[:, None] == head_of_col[None, :]).astype(BF16)
    tril = jnp.tril(jnp.ones((q, q), BF16))
    eye = jnp.eye(LANES, dtype=BF16)
    dskip_cols = jnp.repeat(d_skip, SSM_HEAD_DIM).reshape(1, d_inner)

    zb = d_inner // d_inner
    row = lambda b, c: b * nc + c
    const2 = lambda b, c: (0, 0)
    out_dtype = BF16 if q % 16 == 0 else F32
    yn, ssm_t, conv_tail = pl.pallas_call(
        functools.partial(_ssd_kernel, chunk=q, n_heads=n_heads, d_inner=d_inner, gn_cols=gn_cols),
        grid=(batch, nc),
        in_specs=[
            pl.BlockSpec((q, d_inner), lambda b, c: (row(b, c), 0)),
            pl.BlockSpec((q, d_inner), lambda b, c: (row(b, c), zb)),
            pl.BlockSpec((q, gn_cols), lambda b, c: (row(b, c), 2 * d_inner // gn_cols)),
            pl.BlockSpec((q, gn_cols), lambda b, c: (row(b, c), 2 * d_inner // gn_cols + 1)),
            pl.BlockSpec((q, LANES), lambda b, c: (row(b, c), (2 * d_inner + 2 * gn_cols) // LANES)),
            pl.BlockSpec((1, pad, conv_dim), lambda b, c: (b, 0, 0)),
            pl.BlockSpec((1, SSM_STATE, d_inner), lambda b, c: (b, 0, 0)),
            pl.BlockSpec((CONV_WIDTH, conv_dim), const2),
            pl.BlockSpec((1, conv_dim), const2),
            pl.BlockSpec((1, LANES), const2),
            pl.BlockSpec((1, LANES), const2),
            pl.BlockSpec((1, d_inner), const2),
            pl.BlockSpec((1, d_inner), const2),
            pl.BlockSpec((LANES, d_inner), const2),
            pl.BlockSpec((q, q), const2),
            pl.BlockSpec((LANES, LANES), const2),
        ],
        out_specs=[
            pl.BlockSpec((q, d_inner), lambda b, c: (row(b, c), 0)),
            pl.BlockSpec((1, SSM_STATE, d_inner), lambda b, c: (b, 0, 0)),
            pl.BlockSpec((1, pad, conv_dim), lambda b, c: (b, 0, 0)),
        ],
        out_shape=[
            jax.ShapeDtypeStruct((batch * seq, d_inner), out_dtype),
            jax.ShapeDtypeStruct((batch, SSM_STATE, d_inner), F32),
            jax.ShapeDtypeStruct((batch, pad, conv_dim), F32),
        ],
        scratch_shapes=[
            pltpu.VMEM((q + pad, conv_dim), F32),
            pltpu.VMEM((q, conv_dim), F32),
            pltpu.VMEM((q, d_inner), F32),
            pltpu.VMEM((SSM_STATE, d_inner), F32),
        ],
        compiler_params=_params("parallel", "arbitrary"),
        name=name,
    )(proj, proj, proj, proj, proj, conv0, ssm0, conv_w, conv_b.reshape(1, conv_dim),
      lane_pad(dt_bias), lane_pad(a_log), dskip_cols, gn.reshape(1, d_inner), expand, tril, eye)
    ssm_new = jnp.transpose(ssm_t, (0, 2, 1)).reshape(batch, n_heads, SSM_HEAD_DIM, SSM_STATE)
    return yn, ssm_new, conv_tail[:, pad - (CONV_WIDTH - 1):]


def _router_kernel(x_ref, g_ref, whi_ref, wlo_ref, bias_ref, gates_ref, hn_ref):
    h = _rms_rows(x_ref[...], g_ref[...])
    hn_ref[...] = h.astype(BF16)
    h_hi, h_lo = _split_bf16(h, 2)
    logits = (_dot(h_hi, whi_ref[...]) + _dot(h_lo, whi_ref[...]) + _dot(h_hi, wlo_ref[...])
              + bias_ref[...])
    tm = logits.shape[0]
    lane = lax.broadcasted_iota(jnp.int32, (tm, LANES), 1)
    neg = -jnp.inf
    is_group = lane < N_EXPERT_GROUPS
    gl = jnp.where(is_group, logits, neg)
    g_max = jnp.max(gl, axis=-1, keepdims=True)
    g_idx = jnp.min(jnp.where(gl == g_max, lane, LANES), axis=-1, keepdims=True)
    g_w = 1.0 / jnp.sum(jnp.where(is_group, jnp.exp(gl - g_max), 0.0), axis=-1, keepdims=True)
    first = N_EXPERT_GROUPS + g_idx * EXPERTS_PER_GROUP
    in_group = (lane >= first) & (lane < first + EXPERTS_PER_GROUP)
    el = jnp.where(in_group, logits, neg)
    v1 = jnp.max(el, axis=-1, keepdims=True)
    i1 = jnp.min(jnp.where(el == v1, lane, LANES), axis=-1, keepdims=True)
    el2 = jnp.where(lane == i1, neg, el)
    v2 = jnp.max(el2, axis=-1, keepdims=True)
    i2 = jnp.min(jnp.where(el2 == v2, lane, LANES), axis=-1, keepdims=True)
    e2 = jnp.exp(v2 - v1)
    w1 = g_w / (1.0 + e2)
    w2 = g_w * e2 / (1.0 + e2)
    gates = jnp.where(lane == i1, w1, 0.0) + jnp.where(lane == i2, w2, 0.0)
    gates_ref[...] = pltpu.roll(gates, LANES - N_EXPERT_GROUPS, 1)


def moe_router(x, g, w_group, b_group, w_expert, b_expert, *, tm, name="router"):
    m, k = x.shape
    w = jnp.concatenate([w_group, w_expert.reshape(k, N_EXPERTS)], axis=1)
    w = jnp.pad(w, ((0, 0), (0, LANES - w.shape[1])))
    w_hi = w.astype(BF16)
    w_lo = (w - w_hi.astype(F32)).astype(BF16)
    bias = jnp.pad(jnp.concatenate([b_group, b_expert.reshape(N_EXPERTS)]),
                   (0, LANES - N_EXPERT_GROUPS - N_EXPERTS)).reshape(1, LANES)
    return pl.pallas_call(
        _router_kernel,
        grid=(m // tm,),
        in_specs=[pl.BlockSpec((tm, k), lambda i: (i, 0)),
                  pl.BlockSpec((1, k), lambda i: (0, 0)),
                  pl.BlockSpec((k, LANES), lambda i: (0, 0)),
                  pl.BlockSpec((k, LANES), lambda i: (0, 0)),
                  pl.BlockSpec((1, LANES), lambda i: (0, 0))],
        out_specs=[pl.BlockSpec((tm, LANES), lambda i: (i, 0)),
                   pl.BlockSpec((tm, k), lambda i: (i, 0))],
        out_shape=[jax.ShapeDtypeStruct((m, LANES), F32),
                   jax.ShapeDtypeStruct((m, k), BF16)],
        compiler_params=_params("parallel"),
        name=name,
    )(x, g.reshape(1, k), w_hi, w_lo, bias)


def _moe_kernel(hn_ref, gates_ref, x_ref, wg_ref, wu_ref, wd_ref, o_ref):
    e = pl.program_id(1)

    @pl.when(e == 0)
    def _():
        o_ref[...] = x_ref[...]

    h = hn_ref[...]
    lane = lax.broadcasted_iota(jnp.int32, gates_ref.shape, 1)
    gate = jnp.sum(jnp.where(lane == e, gates_ref[...], 0.0), axis=-1, keepdims=True)
    hidden = _silu(_dot(h, wg_ref[0])) * _dot(h, wu_ref[0]) * gate
    o_ref[...] += _dot(hidden.astype(BF16), wd_ref[0])


def moe_ffn(hn, gates, x, w_gate, w_up, w_down, *, tm, name="moe"):
    m, k = hn.shape
    n_e, _, f = w_gate.shape
    return pl.pallas_call(
        _moe_kernel,
        grid=(m // tm, n_e),
        in_specs=[pl.BlockSpec((tm, k), lambda i, e: (i, 0)),
                  pl.BlockSpec((tm, LANES), lambda i, e: (i, 0)),
                  pl.BlockSpec((tm, k), lambda i, e: (i, 0)),
                  pl.BlockSpec((1, k, f), lambda i, e: (e, 0, 0)),
                  pl.BlockSpec((1, k, f), lambda i, e: (e, 0, 0)),
                  pl.BlockSpec((1, f, k), lambda i, e: (e, 0, 0))],
        out_specs=pl.BlockSpec((tm, k), lambda i, e: (i, 0)),
        out_shape=jax.ShapeDtypeStruct((m, k), F32),
        compiler_params=_params("parallel", "arbitrary"),
        name=name,
    )(hn, gates, x, w_gate, w_up, w_down)


def _logf_kernel(x_ref, g_ref, w_ref, b_ref, tril_ref, eye_ref, logf_ref, cum_t_ref, carry_scr):
    i = pl.program_id(0)

    @pl.when(i == 0)
    def _():
        carry_scr[...] = jnp.zeros_like(carry_scr)

    h = _rms_rows(x_ref[...], g_ref[...]).astype(BF16)
    zf = _dot(h, w_ref[...]) + b_ref[...]
    logf = -_softplus(-zf)
    logf_ref[...] = logf
    cum = sum(_dot(tril_ref[...], t) for t in _split_bf16(logf, 3)) + carry_scr[...]
    tm = cum.shape[0]
    carry_scr[...] = cum[tm - 1:tm, :]
    cum_t_ref[...] = sum(_dot_nt(eye_ref[...], t) for t in _split_bf16(cum, 3))


def logf_proj(x, g, w_f, b_f, *, tm, name="logf"):
    m, k = x.shape
    n_h = w_f.shape[1]
    w = jnp.pad(w_f, ((0, 0), (0, LANES - n_h))).astype(BF16)
    b = jnp.pad(b_f, (0, LANES - n_h)).reshape(1, LANES)
    tril = jnp.tril(jnp.ones((tm, tm), BF16))
    eye = jnp.eye(LANES, dtype=BF16)
    return pl.pallas_call(
        _logf_kernel,
        grid=(m // tm,),
        in_specs=[pl.BlockSpec((tm, k), lambda i: (i, 0)),
                  pl.BlockSpec((1, k), lambda i: (0, 0)),
                  pl.BlockSpec((k, LANES), lambda i: (0, 0)),
                  pl.BlockSpec((1, LANES), lambda i: (0, 0)),
                  pl.BlockSpec((tm, tm), lambda i: (0, 0)),
                  pl.BlockSpec((LANES, LANES), lambda i: (0, 0))],
        out_specs=[pl.BlockSpec((tm, LANES), lambda i: (i, 0)),
                   pl.BlockSpec((LANES, tm), lambda i: (0, i))],
        out_shape=[jax.ShapeDtypeStruct((m, LANES), F32),
                   jax.ShapeDtypeStruct((LANES, m), F32)],
        scratch_shapes=[pltpu.VMEM((1, LANES), F32)],
        compiler_params=_params("arbitrary"),
        name=name,
    )(x, g.reshape(1, k), w, b, tril, eye)


def _flash_kernel(q_ref, k_ref, v_ref, c_ref, o_ref, m_scr, l_scr, acc_scr, *, tq):
    qi = pl.program_id(1)
    q = q_ref[...]
    c_base = c_ref[0, :, pl.ds(pl.multiple_of(qi * tq, tq), 1)]

    m_scr[...] = jnp.full_like(m_scr, -jnp.inf)
    l_scr[...] = jnp.zeros_like(l_scr)
    acc_scr[...] = jnp.zeros_like(acc_scr)

    def block(ki, masked):
        start = pl.multiple_of(ki * tq, tq)
        k = k_ref[pl.ds(start, tq), :]
        v = v_ref[pl.ds(start, tq), :]
        s = _dot_nt(q, k) + (c_base - c_ref[0, :, pl.ds(start, tq)])
        if masked:
            row_i = lax.broadcasted_iota(jnp.int32, (tq, tq), 0)
            col_i = lax.broadcasted_iota(jnp.int32, (tq, tq), 1)
            s = jnp.where(col_i <= row_i, s, -jnp.inf)
        m_prev = m_scr[...]
        m_new = jnp.maximum(m_prev, jnp.max(s, axis=-1, keepdims=True))
        alpha = jnp.exp(m_prev - m_new)
        p = jnp.exp(s - m_new)
        l_scr[...] = l_scr[...] * alpha + jnp.sum(p, axis=-1, keepdims=True)
        acc_scr[...] = acc_scr[...] * alpha + _dot(p.astype(BF16), v)
        m_scr[...] = m_new

    def body(ki, carry):
        block(ki, False)
        return carry

    block(qi, True)
    lax.fori_loop(0, qi, body, 0)
    o_ref[...] = (acc_scr[...] / l_scr[...]).astype(o_ref.dtype)


def flash_attention(q, kv, c_t, *, n_heads, tq, name="flash"):
    seq = q.shape[0]
    d = ATT_HEAD_DIM
    return pl.pallas_call(
        functools.partial(_flash_kernel, tq=tq),
        grid=(n_heads, seq // tq),
        in_specs=[pl.BlockSpec((tq, d), lambda h, i: (i, h)),
                  pl.BlockSpec((seq, d), lambda h, i: (0, h)),
                  pl.BlockSpec((seq, d), lambda h, i: (0, n_heads + h)),
                  pl.BlockSpec((1, 1, seq), lambda h, i: (h, 0, 0))],
        out_specs=pl.BlockSpec((tq, d), lambda h, i: (i, h)),
        out_shape=jax.ShapeDtypeStruct((seq, n_heads * d), BF16),
        scratch_shapes=[pltpu.VMEM((tq, 1), F32), pltpu.VMEM((tq, 1), F32),
                        pltpu.VMEM((tq, d), F32)],
        compiler_params=_params("parallel", "arbitrary"),
        name=name,
    )(q, kv, kv, c_t)


def _paged_attn_kernel(pt_ref, q_ref, kvn_ref, lfn_ref, *rest, n_heads, n_tok, pages_per_step, page):
    kv_refs = rest[:pages_per_step]
    lf_refs = rest[pages_per_step:2 * pages_per_step]
    (expand_t_ref, upper_ref, tril_ref, o_ref, qbd_scr, m_scr, l_scr, acc_scr, carry_scr) = \
        rest[2 * pages_per_step:]
    del pt_ref
    j = pl.program_id(1)
    n_steps = pl.num_programs(1)
    hd = n_heads * ATT_HEAD_DIM
    rows = n_heads * n_tok
    assert rows == LANES and page == LANES

    row_i = lax.broadcasted_iota(jnp.int32, (rows, hd), 0)
    col_i = lax.broadcasted_iota(jnp.int32, (rows, hd), 1)

    def attend(k_bf, v_bf, bias, first):
        s = _dot_nt(qbd_scr[...], k_bf) + bias
        if first:
            m_new = jnp.max(s, axis=-1, keepdims=True)
            p = jnp.exp(s - m_new)
            l_new = jnp.sum(p, axis=-1, keepdims=True)
            pv = _dot(p.astype(BF16), v_bf)
            alpha = None
        else:
            m_prev = m_scr[...]
            m_new = jnp.maximum(m_prev, jnp.max(s, axis=-1, keepdims=True))
            alpha = jnp.exp(m_prev - m_new)
            p = jnp.exp(s - m_new)
            l_new = l_scr[...] * alpha + jnp.sum(p, axis=-1, keepdims=True)
            pv = _dot(p.astype(BF16), v_bf)
        diag = jnp.concatenate(
            [pv[h * n_tok:(h + 1) * n_tok, h * ATT_HEAD_DIM:(h + 1) * ATT_HEAD_DIM]
             for h in range(n_heads)], axis=0)
        if first:
            acc_scr[...] = diag
        else:
            acc_scr[...] = acc_scr[...] * alpha + diag
        m_scr[...] = m_new
        l_scr[...] = l_new

    @pl.when(j == 0)
    def _():
        q_t = jnp.concatenate([q_ref[0]] * n_heads, axis=0)
        same_head = (row_i // n_tok) == (col_i // ATT_HEAD_DIM)
        qbd_scr[...] = jnp.where(same_head, q_t, 0.0).astype(BF16)
        lfn = lfn_ref[0]
        c_new = sum(_dot(tril_ref[...], t) for t in _split_bf16(lfn, 3))
        bias = -sum(_dot_nt(expand_t_ref[...], t) for t in _split_bf16(c_new, 3))
        r = lax.broadcasted_iota(jnp.int32, (rows, page), 0)
        s_i = lax.broadcasted_iota(jnp.int32, (rows, page), 1)
        bias = jnp.where(s_i <= (r % n_tok), bias, -jnp.inf)
        kvn = kvn_ref[0]
        attend(kvn[:, :hd].astype(BF16), kvn[:, hd:].astype(BF16), bias, True)
        carry_scr[...] = jnp.zeros_like(carry_scr)

    for i in range(pages_per_step):
        lf = lf_refs[i][0]
        inner = sum(_dot(upper_ref[...], t) for t in _split_bf16(lf, 3))
        d = inner + carry_scr[...]
        carry_scr[...] = d[0:1, :] + lf[0:1, :]
        bias = sum(_dot_nt(expand_t_ref[...], t) for t in _split_bf16(d, 3))
        kv = kv_refs[i][0]
        attend(kv[:, :hd].astype(BF16), kv[:, hd:].astype(BF16), bias, False)

    @pl.when(j == n_steps - 1)
    def _():
        out = acc_scr[...] / l_scr[...]
        o_ref[0] = jnp.concatenate(
            [out[h * n_tok:(h + 1) * n_tok, :] for h in range(n_heads)], axis=1).astype(o_ref.dtype)


def paged_attention(q, kv_new, logf_new, cache_kv, cache_logf, page_table, *, pages_per_step,
                    name="paged_attn"):
    n_b, n_tok, hd = q.shape
    n_heads = hd // ATT_HEAD_DIM
    n_pool, page = cache_logf.shape[0], cache_logf.shape[1]
    n_pages = page_table.shape[1]
    assert n_pages % pages_per_step == 0
    n_steps = n_pages // pages_per_step
    cache2 = cache_kv.reshape(n_pool, page, 2 * hd)
    lf_cache = jnp.pad(cache_logf, ((0, 0), (0, 0), (0, LANES - n_heads)))
    kvn = jnp.pad(kv_new, ((0, 0), (0, page - n_tok), (0, 0)))
    head_lane = lax.broadcasted_iota(jnp.int32, (n_b, n_tok, LANES), 2) < n_heads
    lfn = jnp.pad(jnp.where(head_lane, logf_new, 0.0), ((0, 0), (0, page - n_tok), (0, 0)))
    rows = n_heads * n_tok
    expand_t = (jnp.arange(rows)[:, None] // n_tok == jnp.arange(LANES)[None, :]).astype(BF16)
    upper = jnp.triu(jnp.ones((page, page), BF16), k=1)
    tril = jnp.tril(jnp.ones((page, page), BF16))

    def page_map(i):
        return lambda b, j, pt: (pt[b, n_pages - 1 - (j * pages_per_step + i)], 0, 0)

    const2 = lambda b, j, pt: (0, 0)
    in_specs = [pl.BlockSpec((1, n_tok, hd), lambda b, j, pt: (b, 0, 0)),
                pl.BlockSpec((1, page, 2 * hd), lambda b, j, pt: (b, 0, 0)),
                pl.BlockSpec((1, page, LANES), lambda b, j, pt: (b, 0, 0))]
    in_specs += [pl.BlockSpec((1, page, 2 * hd), page_map(i)) for i in range(pages_per_step)]
    in_specs += [pl.BlockSpec((1, page, LANES), page_map(i)) for i in range(pages_per_step)]
    in_specs += [pl.BlockSpec((rows, LANES), const2),
                 pl.BlockSpec((page, page), const2),
                 pl.BlockSpec((page, page), const2)]
    grid_spec = pltpu.PrefetchScalarGridSpec(
        num_scalar_prefetch=1,
        grid=(n_b, n_steps),
        in_specs=in_specs,
        out_specs=pl.BlockSpec((1, n_tok, hd), lambda b, j, pt: (b, 0, 0)),
        scratch_shapes=[pltpu.VMEM((rows, hd), BF16),
                        pltpu.VMEM((rows, 1), F32), pltpu.VMEM((rows, 1), F32),
                        pltpu.VMEM((rows, ATT_HEAD_DIM), F32),
                        pltpu.VMEM((1, LANES), F32)])
    return pl.pallas_call(
        functools.partial(_paged_attn_kernel, n_heads=n_heads, n_tok=n_tok,
                          pages_per_step=pages_per_step, page=page),
        grid_spec=grid_spec,
        out_shape=jax.ShapeDtypeStruct((n_b, n_tok, hd), F32),
        compiler_params=_params("parallel", "arbitrary"),
        name=name,
    )(page_table, q, kvn, lfn, *([cache2] * pages_per_step), *([lf_cache] * pages_per_step),
      expand_t, upper, tril)


def _prep_weights(p):
    d_model = p["a_w_in"].shape[1]
    w_in = p["a_w_in"][0]
    n_in = w_in.shape[1]
    d_inner = p["a_w_out"].shape[1]
    n_dt = n_in - (2 * d_inner + 2 * SSM_GROUPS * SSM_STATE)
    w_in = jnp.pad(w_in, ((0, 0), (0, LANES - n_dt))).astype(BF16)
    hd = p["b_w_q"].shape[2]
    return dict(
        w_in=w_in, w_out=p["a_w_out"][0].astype(BF16),
        w_kv=p["w_kv"][:, :2 * hd].astype(BF16), w_f=p["w_kv"][:, 2 * hd:],
        w_q=p["b_w_q"][0].astype(BF16), w_o=p["b_w_o"][0].astype(BF16),
        m_gate=p["m_w_gate"].astype(BF16), m_up=p["m_w_up"].astype(BF16),
        m_down=p["m_w_down"].astype(BF16), d_model=d_model, d_inner=d_inner, hd=hd)


def _moe_layer(x, p, w, layer, tm):
    gates, hn = moe_router(x, p["norm_ffn"][layer], p["m_w_group"][layer], p["m_b_group"][layer],
                           p["m_w_expert"][layer], p["m_b_expert"][layer], tm=tm,
                           name=f"router{layer}")
    return moe_ffn(hn, gates, x, w["m_gate"][layer], w["m_up"][layer], w["m_down"][layer], tm=tm,
                   name=f"moe{layer}")


def _trunk(x, conv_state, ssm_state, p, w, *, batch, seq, tm, attend):
    d_inner, hd = w["d_inner"], w["hd"]
    n_heads = hd // ATT_HEAD_DIM
    n_in = w["w_in"].shape[1]
    tn_in = n_in // 9 if n_in % (9 * LANES) == 0 else LANES
    proj = norm_matmul(x, p["norm_mix"][0], w["w_in"], tm=tm, tn=tn_in, name="in_proj")
    yn, ssm_new, conv_new = ssd_mixer(
        proj, conv_state, ssm_state, p["a_conv_w"][0], p["a_conv_b"][0], p["a_dt_bias"][0],
        p["a_a_log"][0], p["a_d_skip"][0], p["a_norm"][0],
        batch=batch, seq=seq, chunk=min(128, seq), d_inner=d_inner)
    x = matmul_res(yn, w["w_out"], x, tm=tm, tn=512, name="out_proj")
    x = _moe_layer(x, p, w, 0, tm)

    kv = norm_matmul(x, p["kv_norm"], w["w_kv"], tm=tm, tn=512, head_g=p["k_norm"],
                     n_normed_cols=hd, name="kv_proj")
    logf, cum_t = logf_proj(x, p["kv_norm"], w["w_f"], p["b_forget"], tm=tm)
    q = norm_matmul(x, p["norm_mix"][1], w["w_q"], tm=tm, tn=512, head_g=p["b_q_norm"][0],
                    n_normed_cols=hd, out_scale=ATT_HEAD_DIM ** -0.5,
                    out_dtype=BF16 if attend == "prompt" else F32, name="q_proj")
    if attend == "prompt":
        c_t = cum_t[:n_heads].reshape(n_heads, 1, batch * seq)
        o = flash_attention(q, kv.astype(BF16), c_t, n_heads=n_heads, tq=512)
    else:
        cache_kv, cache_logf, page_table = attend
        o = paged_attention(q.reshape(batch, seq, hd), kv.reshape(batch, seq, 2 * hd),
                            logf.reshape(batch, seq, LANES), cache_kv, cache_logf, page_table,
                            pages_per_step=2).reshape(batch * seq, hd)
    x = matmul_res(o, w["w_o"], x, tm=tm, tn=512, name="o_proj")
    x = _moe_layer(x, p, w, 1, tm)
    return x, ssm_new, conv_new, kv, logf[:, :n_heads]


def kernel(x_prompt, x_sample, state_ssm, state_conv, cache_kv, cache_logf, page_table, norm_mix, norm_ffn, a_w_in, a_conv_w, a_conv_b, a_dt_bias, a_a_log, a_d_skip, a_norm, a_w_out, kv_norm, w_kv, b_forget, k_norm, b_w_q, b_q_norm, b_w_o, m_w_group, m_b_group, m_w_expert, m_b_expert, m_w_gate, m_w_up, m_w_down):
    p = dict(norm_mix=norm_mix, norm_ffn=norm_ffn, a_w_in=a_w_in, a_conv_w=a_conv_w,
             a_conv_b=a_conv_b, a_dt_bias=a_dt_bias, a_a_log=a_a_log, a_d_skip=a_d_skip,
             a_norm=a_norm, a_w_out=a_w_out, kv_norm=kv_norm, w_kv=w_kv, b_forget=b_forget,
             k_norm=k_norm, b_w_q=b_w_q, b_q_norm=b_q_norm, b_w_o=b_w_o, m_w_group=m_w_group,
             m_b_group=m_b_group, m_w_expert=m_w_expert, m_b_expert=m_b_expert,
             m_w_gate=m_w_gate, m_w_up=m_w_up, m_w_down=m_w_down)
    assert a_w_in.shape[0] == 1 and b_w_q.shape[0] == 1, "one SSD layer then one attention layer"
    w = _prep_weights(p)
    d_model, d_inner, hd = w["d_model"], w["d_inner"], w["hd"]
    n_heads = hd // ATT_HEAD_DIM
    conv_dim = state_conv.shape[-1]
    ssm_heads = state_ssm.shape[2]

    b_p, seq_p, _ = x_prompt.shape
    assert b_p == 1, "the prompt path carries one cumulative forget sum over its rows"
    conv0 = jnp.zeros((b_p, CONV_WIDTH - 1, conv_dim), F32)
    ssm0 = jnp.zeros((b_p, ssm_heads, SSM_HEAD_DIM, SSM_STATE), F32)
    y_p, ssm_p, conv_p, kv_p, logf_p = _trunk(
        x_prompt.reshape(b_p * seq_p, d_model), conv0, ssm0, p, w,
        batch=b_p, seq=seq_p, tm=512, attend="prompt")

    b_s, seq_s, _ = x_sample.shape
    y_s, ssm_s, conv_s, kv_s, logf_s = _trunk(
        x_sample.reshape(b_s * seq_s, d_model), state_conv[0], state_ssm[0], p, w,
        batch=b_s, seq=seq_s, tm=b_s * seq_s, attend=(cache_kv, cache_logf, page_table))

    return (y_p.reshape(b_p, seq_p, d_model), y_s.reshape(b_s, seq_s, d_model),
            ssm_p[None], conv_p[None],
            kv_p.reshape(b_p, seq_p, 2, n_heads, ATT_HEAD_DIM), logf_p.reshape(b_p, seq_p, n_heads),
            ssm_s[None], conv_s[None],
            kv_s.reshape(b_s, seq_s, 2, n_heads, ATT_HEAD_DIM), logf_s.reshape(b_s, seq_s, n_heads))
```

```python
import functools

import jax
import jax.numpy as jnp
from jax import lax
from jax.experimental import pallas as pl
from jax.experimental.pallas import tpu as pltpu

F32 = jnp.float32
BF16 = jnp.bfloat16

LANES = 128
SUBLANES = 8
VMEM_LIMIT = 56 * 1024 * 1024

RMS_EPS = 1e-5
LOG2E = 1.4426950408889634
SSM_HEAD_DIM = 64
SSM_GROUPS = 8
SSM_STATE = 128
CONV_WIDTH = 4
ATT_HEAD_DIM = 128
N_EXPERT_GROUPS = 4
EXPERTS_PER_GROUP = 4
N_EXPERTS = N_EXPERT_GROUPS * EXPERTS_PER_GROUP


def _params(*sem):
    return pltpu.CompilerParams(dimension_semantics=sem, vmem_limit_bytes=VMEM_LIMIT)


def _dot(a, b):
    return jnp.dot(a, b, preferred_element_type=F32)


def _dot_nt(a, b):
    return lax.dot_general(a, b, (((1,), (1,)), ((), ())), preferred_element_type=F32)


def _dot_tn(a, b):
    return lax.dot_general(a, b, (((0,), (0,)), ((), ())), preferred_element_type=F32)


def _split_bf16(v, terms):
    out = []
    r = v
    for _ in range(terms):
        t = r.astype(BF16)
        out.append(t)
        r = r - t.astype(F32)
    return out


def _mm(a, b, dot, precise):
    if not precise:
        return dot(a.astype(BF16), b.astype(BF16))
    a_hi, a_lo = _split_bf16(a, 2)
    b_hi, b_lo = _split_bf16(b, 2)
    return dot(a_hi, b_hi) + (dot(a_lo, b_hi) + dot(a_hi, b_lo))


def _silu(x):
    return x / (1.0 + jnp.exp(-x))


def _softplus(x):
    return jnp.maximum(x, 0.0) + jnp.log1p(jnp.exp(-jnp.abs(x)))


def _rms_rows(x, g):
    ms = jnp.mean(x * x, axis=-1, keepdims=True)
    return x * lax.rsqrt(ms + RMS_EPS) * g


def _norm_matmul_kernel(*refs, n_normed_tiles, has_head_norm, out_scale, bf16_copy, precise):
    refs = list(refs)
    h_scr = refs.pop()
    o2_ref = refs.pop() if bf16_copy else None
    o_ref = refs.pop()
    hg_ref = refs.pop() if has_head_norm else None
    x_ref, g_ref, w_ref = refs
    j = pl.program_id(1)

    @pl.when(j == 0)
    def _():
        h_scr[...] = _rms_rows(x_ref[...], g_ref[...]).astype(h_scr.dtype)

    acc = _mm(h_scr[...], w_ref[...], _dot, precise)
    tn = acc.shape[1]

    def head_normed(a):
        parts = []
        for c in range(tn // ATT_HEAD_DIM):
            blk = a[:, c * ATT_HEAD_DIM:(c + 1) * ATT_HEAD_DIM]
            parts.append(_rms_rows(blk, hg_ref[...]))
        return jnp.concatenate(parts, axis=1)

    def emit(a):
        if out_scale != 1.0:
            a = a * out_scale
        o_ref[...] = a.astype(o_ref.dtype)
        if bf16_copy:
            o2_ref[...] = a.astype(BF16)

    if not has_head_norm:
        emit(acc)
    else:
        @pl.when(j < n_normed_tiles)
        def _():
            emit(head_normed(acc))

        @pl.when(j >= n_normed_tiles)
        def _():
            emit(acc)


def norm_matmul(x, g, w, *, tm, tn, head_g=None, n_normed_cols=0, out_scale=1.0, out_dtype=F32,
                bf16_copy=False, precise=False, name="norm_matmul"):
    m, k = x.shape
    n = w.shape[1]
    assert m % tm == 0 and n % tn == 0 and n_normed_cols % tn == 0
    has_hn = head_g is not None
    in_specs = [pl.BlockSpec((tm, k), lambda i, j: (i, 0)),
                pl.BlockSpec((1, k), lambda i, j: (0, 0)),
                pl.BlockSpec((k, tn), lambda i, j: (0, j))]
    args = [x, g.reshape(1, k), w]
    if has_hn:
        in_specs.append(pl.BlockSpec((1, ATT_HEAD_DIM), lambda i, j: (0, 0)))
        args.append(head_g.reshape(1, ATT_HEAD_DIM))
    out_specs = [pl.BlockSpec((tm, tn), lambda i, j: (i, j))] * 2
    out_shape = [jax.ShapeDtypeStruct((m, n), out_dtype), jax.ShapeDtypeStruct((m, n), BF16)]
    return pl.pallas_call(
        functools.partial(_norm_matmul_kernel, n_normed_tiles=n_normed_cols // tn,
                          has_head_norm=has_hn, out_scale=out_scale, bf16_copy=bf16_copy,
                          precise=precise),
        grid=(m // tm, n // tn),
        in_specs=in_specs,
        out_specs=out_specs if bf16_copy else out_specs[0],
        out_shape=out_shape if bf16_copy else out_shape[0],
        scratch_shapes=[pltpu.VMEM((tm, k), F32 if precise else BF16)],
        compiler_params=_params("parallel", "arbitrary"),
        name=name,
    )(*args)


def _matmul_res_kernel(a_ref, w_ref, r_ref, o_ref, *, precise):
    o_ref[...] = r_ref[...] + _mm(a_ref[...], w_ref[...], _dot, precise)


def matmul_res(a, w, res, *, tm, tn, precise=False, name="matmul_res"):
    m, k = a.shape
    n = w.shape[1]
    assert m % tm == 0 and n % tn == 0
    return pl.pallas_call(
        functools.partial(_matmul_res_kernel, precise=precise),
        grid=(m // tm, n // tn),
        in_specs=[pl.BlockSpec((tm, k), lambda i, j: (i, 0)),
                  pl.BlockSpec((k, tn), lambda i, j: (0, j)),
                  pl.BlockSpec((tm, tn), lambda i, j: (i, j))],
        out_specs=pl.BlockSpec((tm, tn), lambda i, j: (i, j)),
        out_shape=jax.ShapeDtypeStruct((m, n), F32),
        compiler_params=_params("parallel", "arbitrary"),
        name=name,
    )(a, w, res)


def _ssd_kernel(z_ref, xs_ref, b_ref, c_ref, dt_ref, conv0_ref, ssm0_ref, cw_ref, cbias_ref,
                dtb_ref, alog_ref, dskip_ref, gn_ref, expand_ref, tril_ref, eye_ref,
                yn_ref, ssm_out_ref, conv_out_ref,
                ext_scr, act_scr, y_scr, h_scr, *, chunk, n_heads, d_inner, gn_cols, precise):
    q = chunk
    c = pl.program_id(1)
    nc = pl.num_programs(1)
    pad = SUBLANES
    heads_per_group = n_heads // SSM_GROUPS
    gcols = heads_per_group * SSM_HEAD_DIM

    @pl.when(c == 0)
    def _():
        ext_scr[0:pad, :] = conv0_ref[0]
        h_scr[...] = ssm0_ref[0]

    ext_scr[pad:pad + q, 0:d_inner] = xs_ref[...]
    ext_scr[pad:pad + q, d_inner:d_inner + gn_cols] = b_ref[...]
    ext_scr[pad:pad + q, d_inner + gn_cols:] = c_ref[...]

    acc = jnp.broadcast_to(cbias_ref[...], (q, ext_scr.shape[1]))
    for tap in range(CONV_WIDTH):
        start = pad - (CONV_WIDTH - 1) + tap
        acc = acc + ext_scr[start:start + q, :] * cw_ref[tap:tap + 1, :]
    act_scr[...] = _silu(acc)

    tail = ext_scr[q:q + pad, :]
    conv_out_ref[0] = tail
    ext_scr[0:pad, :] = tail

    dt = _softplus(dt_ref[...] + dtb_ref[...])
    a = dt * (-jnp.exp(alog_ref[...]))
    tril = tril_ref[...]
    eye = eye_ref[...]
    a_cum = sum(_dot(tril, t) for t in _split_bf16(a, 3))
    a_cum_t = sum(_dot_nt(eye, t) for t in _split_bf16(a_cum, 3))
    dt_t = sum(_dot_nt(eye, t) for t in _split_bf16(dt, 3))
    a_last = a_cum[q - 1:q, :]
    w_state = dt * jnp.exp(a_last - a_cum)
    e_cum = jnp.exp(a_cum)
    expand = expand_ref[...]
    w_state_x = sum(_dot(t, expand) for t in _split_bf16(w_state, 2))
    e_cum_x = sum(_dot(t, expand) for t in _split_bf16(e_cum, 2))

    row_i = lax.broadcasted_iota(jnp.int32, (q, q), 0)
    col_i = lax.broadcasted_iota(jnp.int32, (q, q), 1)
    causal = col_i <= row_i
    lane = lax.broadcasted_iota(jnp.int32, (q, LANES), 1)
    lo_half = lane < SSM_HEAD_DIM

    operand = (lambda v: v) if precise else (lambda v: v.astype(BF16))
    for g in range(SSM_GROUPS):
        bg = operand(act_scr[:, d_inner + g * SSM_STATE:d_inner + (g + 1) * SSM_STATE])
        cg = operand(act_scr[:, d_inner + gn_cols + g * SSM_STATE:
                             d_inner + gn_cols + (g + 1) * SSM_STATE])
        cb = _mm(cg, bg, _dot_nt, precise)
        for j in range(gcols // LANES):
            c0 = g * gcols + j * LANES
            xs_blk = act_scr[:, c0:c0 + LANES]
            y_pair = None
            for half in range(2):
                h = (c0 // SSM_HEAD_DIM) + half
                seg = a_cum[:, h:h + 1] - a_cum_t[h:h + 1, :]
                m = cb * jnp.exp(jnp.where(causal, seg, -jnp.inf)) * dt_t[h:h + 1, :]
                keep = lo_half if half == 0 else jnp.logical_not(lo_half)
                part = _mm(m, jnp.where(keep, xs_blk, 0.0), _dot, precise)
                y_pair = part if y_pair is None else y_pair + part
            y_scr[:, c0:c0 + LANES] = y_pair
        gs = slice(g * gcols, (g + 1) * gcols)
        h_g = h_scr[:, gs]
        y_off = _mm(cg, h_g, _dot, precise) * e_cum_x[:, gs]
        y_scr[:, gs] = y_scr[:, gs] + y_off
        xw = act_scr[:, gs] * w_state_x[:, gs]
        h_scr[:, gs] = h_g * e_cum_x[q - 1:q, gs] + _mm(bg, xw, _dot_tn, precise)

    y = y_scr[...] + act_scr[:, 0:d_inner] * dskip_ref[...]
    y = y * _silu(z_ref[...])
    gw = d_inner // SSM_GROUPS
    parts = []
    for g in range(SSM_GROUPS):
        parts.append(_rms_rows(y[:, g * gw:(g + 1) * gw], gn_ref[:, g * gw:(g + 1) * gw]))
    yn_ref[...] = jnp.concatenate(parts, axis=1).astype(yn_ref.dtype)

    @pl.when(c == nc - 1)
    def _():
        ssm_out_ref[0] = h_scr[...]


def ssd_mixer(proj, conv_state, ssm_state, conv_w, conv_b, dt_bias, a_log, d_skip, gn,
              *, batch, seq, chunk, d_inner, precise=False, name="ssd"):
    n_heads = d_inner // SSM_HEAD_DIM
    gn_cols = SSM_GROUPS * SSM_STATE
    conv_dim = d_inner + 2 * gn_cols
    nc = seq // chunk
    assert seq % chunk == 0 and n_heads <= LANES
    q = chunk
    pad = SUBLANES

    conv0 = jnp.concatenate(
        [jnp.zeros((batch, pad - (CONV_WIDTH - 1), conv_dim), F32), conv_state], axis=1)
    ssm0 = jnp.transpose(ssm_state.reshape(batch, d_inner, SSM_STATE), (0, 2, 1))

    def lane_pad(v):
        return jnp.pad(v, (0, LANES - v.shape[0])).reshape(1, LANES)

    head_of_col = jnp.arange(d_inner) // SSM_HEAD_DIM
    expand = (jnp.arange(LANES)[:, None] == head_of_col[None, :]).astype(BF16)
    tril = jnp.tril(jnp.ones((q, q), BF16))
    eye = jnp.eye(LANES, dtype=BF16)
    dskip_cols = jnp.repeat(d_skip, SSM_HEAD_DIM).reshape(1, d_inner)

    zb = d_inner // d_inner
    row = lambda b, c: b * nc + c
    const2 = lambda b, c: (0, 0)
    out_dtype = BF16 if (q % 16 == 0 and not precise) else F32
    yn, ssm_t, conv_tail = pl.pallas_call(
        functools.partial(_ssd_kernel, chunk=q, n_heads=n_heads, d_inner=d_inner, gn_cols=gn_cols,
                          precise=precise),
        grid=(batch, nc),
        in_specs=[
            pl.BlockSpec((q, d_inner), lambda b, c: (row(b, c), 0)),
            pl.BlockSpec((q, d_inner), lambda b, c: (row(b, c), zb)),
            pl.BlockSpec((q, gn_cols), lambda b, c: (row(b, c), 2 * d_inner // gn_cols)),
            pl.BlockSpec((q, gn_cols), lambda b, c: (row(b, c), 2 * d_inner // gn_cols + 1)),
            pl.BlockSpec((q, LANES), lambda b, c: (row(b, c), (2 * d_inner + 2 * gn_cols) // LANES)),
            pl.BlockSpec((1, pad, conv_dim), lambda b, c: (b, 0, 0)),
            pl.BlockSpec((1, SSM_STATE, d_inner), lambda b, c: (b, 0, 0)),
            pl.BlockSpec((CONV_WIDTH, conv_dim), const2),
            pl.BlockSpec((1, conv_dim), const2),
            pl.BlockSpec((1, LANES), const2),
            pl.BlockSpec((1, LANES), const2),
            pl.BlockSpec((1, d_inner), const2),
            pl.BlockSpec((1, d_inner), const2),
            pl.BlockSpec((LANES, d_inner), const2),
            pl.BlockSpec((q, q), const2),
            pl.BlockSpec((LANES, LANES), const2),
        ],
        out_specs=[
            pl.BlockSpec((q, d_inner), lambda b, c: (row(b, c), 0)),
            pl.BlockSpec((1, SSM_STATE, d_inner), lambda b, c: (b, 0, 0)),
            pl.BlockSpec((1, pad, conv_dim), lambda b, c: (b, 0, 0)),
        ],
        out_shape=[
            jax.ShapeDtypeStruct((batch * seq, d_inner), out_dtype),
            jax.ShapeDtypeStruct((batch, SSM_STATE, d_inner), F32),
            jax.ShapeDtypeStruct((batch, pad, conv_dim), F32),
        ],
        scratch_shapes=[
            pltpu.VMEM((q + pad, conv_dim), F32),
            pltpu.VMEM((q, conv_dim), F32),
            pltpu.VMEM((q, d_inner), F32),
            pltpu.VMEM((SSM_STATE, d_inner), F32),
        ],
        compiler_params=_params("parallel", "arbitrary"),
        name=name,
    )(proj, proj, proj, proj, proj, conv0, ssm0, conv_w, conv_b.reshape(1, conv_dim),
      lane_pad(dt_bias), lane_pad(a_log), dskip_cols, gn.reshape(1, d_inner), expand, tril, eye)
    ssm_new = jnp.transpose(ssm_t, (0, 2, 1)).reshape(batch, n_heads, SSM_HEAD_DIM, SSM_STATE)
    return yn, ssm_new, conv_tail[:, pad - (CONV_WIDTH - 1):]


def _router_kernel(x_ref, g_ref, whi_ref, wlo_ref, bias_ref, gates_ref, hn_ref):
    h = _rms_rows(x_ref[...], g_ref[...])
    hn_ref[...] = h.astype(BF16)
    h_hi, h_lo = _split_bf16(h, 2)
    logits = (_dot(h_hi, whi_ref[...]) + _dot(h_lo, whi_ref[...]) + _dot(h_hi, wlo_ref[...])
              + bias_ref[...])
    tm = logits.shape[0]
    lane = lax.broadcasted_iota(jnp.int32, (tm, LANES), 1)
    neg = -jnp.inf
    is_group = lane < N_EXPERT_GROUPS
    gl = jnp.where(is_group, logits, neg)
    g_max = jnp.max(gl, axis=-1, keepdims=True)
    g_idx = jnp.min(jnp.where(gl == g_max, lane, LANES), axis=-1, keepdims=True)
    g_w = 1.0 / jnp.sum(jnp.where(is_group, jnp.exp(gl - g_max), 0.0), axis=-1, keepdims=True)
    first = N_EXPERT_GROUPS + g_idx * EXPERTS_PER_GROUP
    in_group = (lane >= first) & (lane < first + EXPERTS_PER_GROUP)
    el = jnp.where(in_group, logits, neg)
    v1 = jnp.max(el, axis=-1, keepdims=True)
    i1 = jnp.min(jnp.where(el == v1, lane, LANES), axis=-1, keepdims=True)
    el2 = jnp.where(lane == i1, neg, el)
    v2 = jnp.max(el2, axis=-1, keepdims=True)
    i2 = jnp.min(jnp.where(el2 == v2, lane, LANES), axis=-1, keepdims=True)
    e2 = jnp.exp(v2 - v1)
    w1 = g_w / (1.0 + e2)
    w2 = g_w * e2 / (1.0 + e2)
    gates = jnp.where(lane == i1, w1, 0.0) + jnp.where(lane == i2, w2, 0.0)
    gates_ref[...] = pltpu.roll(gates, LANES - N_EXPERT_GROUPS, 1)


def moe_router(x, g, w_group, b_group, w_expert, b_expert, *, tm, name="router"):
    m, k = x.shape
    w = jnp.concatenate([w_group, w_expert.reshape(k, N_EXPERTS)], axis=1)
    w = jnp.pad(w, ((0, 0), (0, LANES - w.shape[1])))
    w_hi = w.astype(BF16)
    w_lo = (w - w_hi.astype(F32)).astype(BF16)
    bias = jnp.pad(jnp.concatenate([b_group, b_expert.reshape(N_EXPERTS)]),
                   (0, LANES - N_EXPERT_GROUPS - N_EXPERTS)).reshape(1, LANES)
    return pl.pallas_call(
        _router_kernel,
        grid=(m // tm,),
        in_specs=[pl.BlockSpec((tm, k), lambda i: (i, 0)),
                  pl.BlockSpec((1, k), lambda i: (0, 0)),
                  pl.BlockSpec((k, LANES), lambda i: (0, 0)),
                  pl.BlockSpec((k, LANES), lambda i: (0, 0)),
                  pl.BlockSpec((1, LANES), lambda i: (0, 0))],
        out_specs=[pl.BlockSpec((tm, LANES), lambda i: (i, 0)),
                   pl.BlockSpec((tm, k), lambda i: (i, 0))],
        out_shape=[jax.ShapeDtypeStruct((m, LANES), F32),
                   jax.ShapeDtypeStruct((m, k), BF16)],
        compiler_params=_params("parallel"),
        name=name,
    )(x, g.reshape(1, k), w_hi, w_lo, bias)


def _moe_kernel(hn_ref, gates_ref, x_ref, wg_ref, wu_ref, wd_ref, o_ref):
    e = pl.program_id(1)

    @pl.when(e == 0)
    def _():
        o_ref[...] = x_ref[...]

    h = hn_ref[...]
    lane = lax.broadcasted_iota(jnp.int32, gates_ref.shape, 1)
    gate = jnp.sum(jnp.where(lane == e, gates_ref[...], 0.0), axis=-1, keepdims=True)
    hidden = _silu(_dot(h, wg_ref[0])) * _dot(h, wu_ref[0]) * gate
    o_ref[...] += _dot(hidden.astype(BF16), wd_ref[0])


def moe_ffn(hn, gates, x, w_gate, w_up, w_down, *, tm, name="moe"):
    m, k = hn.shape
    n_e, _, f = w_gate.shape
    return pl.pallas_call(
        _moe_kernel,
        grid=(m // tm, n_e),
        in_specs=[pl.BlockSpec((tm, k), lambda i, e: (i, 0)),
                  pl.BlockSpec((tm, LANES), lambda i, e: (i, 0)),
                  pl.BlockSpec((tm, k), lambda i, e: (i, 0)),
                  pl.BlockSpec((1, k, f), lambda i, e: (e, 0, 0)),
                  pl.BlockSpec((1, k, f), lambda i, e: (e, 0, 0)),
                  pl.BlockSpec((1, f, k), lambda i, e: (e, 0, 0))],
        out_specs=pl.BlockSpec((tm, k), lambda i, e: (i, 0)),
        out_shape=jax.ShapeDtypeStruct((m, k), F32),
        compiler_params=_params("parallel", "arbitrary"),
        name=name,
    )(hn, gates, x, w_gate, w_up, w_down)


def _logf_kernel(x_ref, g_ref, w_ref, b_ref, *rest, n_heads, with_bias_cols):
    h = _rms_rows(x_ref[...], g_ref[...]).astype(BF16)
    zf = _dot(h, w_ref[...]) + b_ref[...]
    logf = -_softplus(-zf)
    if not with_bias_cols:
        (logf_ref,) = rest
        logf_ref[...] = logf
        return
    tril_ref, logf_ref, bias_ref, carry_scr = rest
    logf_ref[...] = logf

    @pl.when(pl.program_id(0) == 0)
    def _():
        carry_scr[...] = jnp.zeros_like(carry_scr)

    cum = sum(_dot(tril_ref[...], t) for t in _split_bf16(logf, 3)) + carry_scr[...]
    tm = cum.shape[0]
    carry_scr[...] = cum[tm - 1:tm, :]
    lane = lax.broadcasted_iota(jnp.int32, (tm, LANES), 1)
    for hd in range(n_heads):
        t0, t1, t2 = [t.astype(F32) for t in _split_bf16(cum[:, hd:hd + 1] * (-LOG2E), 3)]
        blk = jnp.where(lane == 0, t0, jnp.where(lane == 1, t1, jnp.where(lane == 2, t2, 0.0)))
        bias_ref[:, hd * LANES:(hd + 1) * LANES] = blk.astype(BF16)


def logf_proj(x, g, w_f, b_f, *, tm, with_bias_cols, name="logf"):
    m, k = x.shape
    n_h = w_f.shape[1]
    w = jnp.pad(w_f, ((0, 0), (0, LANES - n_h))).astype(BF16)
    b = jnp.pad(b_f, (0, LANES - n_h)).reshape(1, LANES)
    in_specs = [pl.BlockSpec((tm, k), lambda i: (i, 0)),
                pl.BlockSpec((1, k), lambda i: (0, 0)),
                pl.BlockSpec((k, LANES), lambda i: (0, 0)),
                pl.BlockSpec((1, LANES), lambda i: (0, 0))]
    args = [x, g.reshape(1, k), w, b]
    out_specs = [pl.BlockSpec((tm, LANES), lambda i: (i, 0))]
    out_shape = [jax.ShapeDtypeStruct((m, LANES), F32)]
    scratch = []
    if with_bias_cols:
        in_specs.append(pl.BlockSpec((tm, tm), lambda i: (0, 0)))
        args.append(jnp.tril(jnp.ones((tm, tm), BF16)))
        out_specs.append(pl.BlockSpec((tm, n_h * LANES), lambda i: (i, 0)))
        out_shape.append(jax.ShapeDtypeStruct((m, n_h * LANES), BF16))
        scratch.append(pltpu.VMEM((1, LANES), F32))
    return pl.pallas_call(
        functools.partial(_logf_kernel, n_heads=n_h, with_bias_cols=with_bias_cols),
        grid=(m // tm,),
        in_specs=in_specs,
        out_specs=out_specs,
        out_shape=out_shape,
        scratch_shapes=scratch,
        compiler_params=_params("arbitrary"),
        name=name,
    )(*args)


def _flash_kernel(q_ref, k_ref, v_ref, bias_ref, o_ref, vt_scr, m_scr, l_scr, acc_scr, *,
                  tq, heads_per_step):
    qi = pl.program_id(1)
    d = ATT_HEAD_DIM
    n_blocks = k_ref.shape[0] // tq

    @pl.when(qi == 0)
    def _():
        for hh in range(heads_per_step):
            for c in range(n_blocks):
                blk = v_ref[c * tq:(c + 1) * tq, hh * d:(hh + 1) * d].astype(F32)
                vt_scr[hh, c] = blk.T.astype(BF16)

    lane = lax.broadcasted_iota(jnp.int32, (tq, d), 1)
    ones_cols = jnp.where(lane < 3, 1.0, 0.0).astype(BF16)
    q_aug = [jnp.concatenate([q_ref[:, hh * d:(hh + 1) * d], ones_cols], axis=1)
             for hh in range(heads_per_step)]

    def block(ki, n_sub, masked, first):
        for hh in range(heads_per_step):
            scores = []
            for u in range(n_sub):
                start = pl.multiple_of((ki + u) * tq, tq)
                k_aug = jnp.concatenate([k_ref[pl.ds(start, tq), hh * d:(hh + 1) * d],
                                         bias_ref[pl.ds(start, tq), hh * d:(hh + 1) * d]], axis=1)
                s = _dot_nt(k_aug, q_aug[hh])
                if masked:
                    key_i = lax.broadcasted_iota(jnp.int32, (tq, tq), 0)
                    qry_i = lax.broadcasted_iota(jnp.int32, (tq, tq), 1)
                    s = jnp.where(key_i <= qry_i, s, -jnp.inf)
                scores.append(s)
            s_max = functools.reduce(jnp.maximum,
                                     [jnp.max(s, axis=0, keepdims=True) for s in scores])
            if first:
                m_new = s_max
            else:
                m_prev = m_scr[hh]
                m_new = jnp.maximum(m_prev, s_max)
                alpha = jnp.exp2(m_prev - m_new)
            probs = [jnp.exp2(s - m_new) for s in scores]
            l_new = sum(jnp.sum(p, axis=0, keepdims=True) for p in probs)
            pv = sum(_dot(vt_scr[hh, ki + u], p.astype(BF16)) for u, p in enumerate(probs))
            if first:
                l_scr[hh] = l_new
                acc_scr[hh] = pv
            else:
                l_scr[hh] = l_scr[hh] * alpha + l_new
                acc_scr[hh] = acc_scr[hh] * alpha + pv
            m_scr[hh] = m_new

    def pair_body(kp, carry):
        block(2 * kp, 2, False, False)
        return carry

    block(qi, 1, True, True)
    lax.fori_loop(0, qi // 2, pair_body, 0)

    @pl.when(qi % 2 == 1)
    def _():
        block(qi - 1, 1, False, False)
    for hh in range(heads_per_step):
        out_t = acc_scr[hh] / l_scr[hh]
        o_ref[:, hh * d:(hh + 1) * d] = out_t.T.astype(o_ref.dtype)


def flash_attention(q, kv, bias_cols, *, n_heads, tq, heads_per_step, name="flash"):
    seq = q.shape[0]
    d = ATT_HEAD_DIM
    w = heads_per_step * d
    n_steps = n_heads // heads_per_step
    assert n_heads % heads_per_step == 0 and seq % tq == 0
    return pl.pallas_call(
        functools.partial(_flash_kernel, tq=tq, heads_per_step=heads_per_step),
        grid=(n_steps, seq // tq),
        in_specs=[pl.BlockSpec((tq, w), lambda h, i: (i, h)),
                  pl.BlockSpec((seq, w), lambda h, i: (0, h)),
                  pl.BlockSpec((seq, w), lambda h, i: (0, n_steps + h)),
                  pl.BlockSpec((seq, w), lambda h, i: (0, h))],
        out_specs=pl.BlockSpec((tq, w), lambda h, i: (i, h)),
        out_shape=jax.ShapeDtypeStruct((seq, n_heads * d), BF16),
        scratch_shapes=[pltpu.VMEM((heads_per_step, seq // tq, d, tq), BF16),
                        pltpu.VMEM((heads_per_step, 1, tq), F32),
                        pltpu.VMEM((heads_per_step, 1, tq), F32),
                        pltpu.VMEM((heads_per_step, d, tq), F32)],
        compiler_params=_params("parallel", "arbitrary"),
        name=name,
    )(q, kv, kv, bias_cols)


def _forget_bias_kernel(pt_ref, lf_ref, lfn_ref, upper_ref, tril_ref, d_ref, cn_ref, carry_scr):
    del pt_ref

    @pl.when(pl.program_id(1) == 0)
    def _():
        carry_scr[...] = jnp.zeros_like(carry_scr)
        c_new = sum(_dot(tril_ref[...], t) for t in _split_bf16(lfn_ref[0], 3))
        cn_ref[0] = c_new * (-LOG2E)

    lf = lf_ref[0]
    inner = sum(_dot(upper_ref[...], t) for t in _split_bf16(lf, 3))
    d = inner + carry_scr[...]
    carry_scr[...] = d[0:1, :] + lf[0:1, :]
    d_ref[0, 0] = d * LOG2E


def forget_bias(cache_logf, page_table, logf_new, name="forget_bias"):
    n_pool, page, n_heads = cache_logf.shape
    n_b, n_pages = page_table.shape
    t_pad = logf_new.shape[1]
    upper = jnp.triu(jnp.ones((page, page), BF16), k=1)
    tril = jnp.tril(jnp.ones((t_pad, t_pad), BF16))
    grid_spec = pltpu.PrefetchScalarGridSpec(
        num_scalar_prefetch=1,
        grid=(n_b, n_pages),
        in_specs=[pl.BlockSpec((1, page, n_heads), lambda b, j, pt: (pt[b, n_pages - 1 - j], 0, 0)),
                  pl.BlockSpec((1, t_pad, n_heads), lambda b, j, pt: (b, 0, 0)),
                  pl.BlockSpec((page, page), lambda b, j, pt: (0, 0)),
                  pl.BlockSpec((t_pad, t_pad), lambda b, j, pt: (0, 0))],
        out_specs=[pl.BlockSpec((1, 1, page, n_heads), lambda b, j, pt: (b, n_pages - 1 - j, 0, 0)),
                   pl.BlockSpec((1, t_pad, n_heads), lambda b, j, pt: (b, 0, 0))],
        scratch_shapes=[pltpu.VMEM((1, n_heads), F32)])
    return pl.pallas_call(
        _forget_bias_kernel,
        grid_spec=grid_spec,
        out_shape=[jax.ShapeDtypeStruct((n_b, n_pages, page, n_heads), F32),
                   jax.ShapeDtypeStruct((n_b, t_pad, n_heads), F32)],
        compiler_params=_params("parallel", "arbitrary"),
        name=name,
    )(page_table, cache_logf, logf_new, upper, tril)


def _paged_attn_kernel(pt_ref, q_ref, kvn_ref, bn_ref, mask_new_ref, mask_page_ref, *rest,
                       n_heads, pages_per_step):
    kv_refs = rest[:pages_per_step]
    b_refs = rest[pages_per_step:2 * pages_per_step]
    o_ref, m_scr, l_scr, acc_scr = rest[2 * pages_per_step:]
    del pt_ref
    j = pl.program_id(1)
    n_steps = pl.num_programs(1)
    n_half = n_heads // SUBLANES

    def rows_of(kv_view, first_head):
        n_keys = kv_view.shape[0]
        blk = kv_view[:, first_head:first_head + SUBLANES, :]
        return blk.reshape(n_keys * SUBLANES, ATT_HEAD_DIM).astype(BF16)

    def attend(kv_views, half, bias_rows, mask, first):
        lo = half * SUBLANES
        scores = [_dot_nt(q_ref[0, half], rows_of(view, lo)) + mask + bias
                  for view, bias in zip(kv_views, bias_rows)]
        s_max = functools.reduce(jnp.maximum, [jnp.max(s, axis=-1, keepdims=True) for s in scores])
        if first:
            m_new = s_max
        else:
            m_prev = m_scr[half]
            m_new = jnp.maximum(m_prev, s_max)
            alpha = jnp.exp2(m_prev - m_new)
        probs = [jnp.exp2(s - m_new) for s in scores]
        l_new = sum(jnp.sum(p, axis=-1, keepdims=True) for p in probs)
        pv = sum(_dot(p.astype(BF16), rows_of(view, n_heads + lo))
                 for p, view in zip(probs, kv_views))
        if first:
            l_scr[half] = l_new
            acc_scr[half] = pv
        else:
            l_scr[half] = l_scr[half] * alpha + l_new
            acc_scr[half] = acc_scr[half] * alpha + pv
        m_scr[half] = m_new

    @pl.when(j == 0)
    def _():
        for half in range(n_half):
            attend([kvn_ref.at[0]], half, [bn_ref[0, half]], mask_new_ref[...], True)

    for half in range(n_half):
        attend([r.at[0] for r in kv_refs], half, [b[0, 0, half] for b in b_refs],
               mask_page_ref[...], False)

    @pl.when(j == n_steps - 1)
    def _():
        for half in range(n_half):
            o_ref[0, half] = acc_scr[half] / l_scr[half]


def paged_attention(q, kv_new, logf_new, cache_kv, cache_logf, page_table, *, pages_per_step,
                    name="paged_attn"):
    n_b, n_tok, hd = q.shape
    d = ATT_HEAD_DIM
    n_heads = hd // d
    n_half = n_heads // SUBLANES
    n_pool, page = cache_logf.shape[0], cache_logf.shape[1]
    n_pages = page_table.shape[1]
    rows = SUBLANES * n_tok
    t_pad = -(-n_tok // 16) * 16
    assert n_pages % pages_per_step == 0 and n_heads % SUBLANES == 0 and rows % 16 == 0
    n_steps = n_pages // pages_per_step

    d_pages, neg_c_new = forget_bias(
        cache_logf, page_table, jnp.pad(logf_new, ((0, 0), (0, t_pad - n_tok), (0, 0))))
    bias_pages = d_pages.reshape(n_b, n_pages, page, n_half, SUBLANES).transpose(0, 1, 3, 2, 4)
    bias_pages = bias_pages.reshape(n_b, n_pages, n_half, 1, page * SUBLANES)
    bias_new = neg_c_new.reshape(n_b, t_pad, n_half, SUBLANES).transpose(0, 2, 1, 3)
    bias_new = bias_new.reshape(n_b, n_half, 1, t_pad * SUBLANES)

    row_head = jnp.arange(rows)[:, None] // n_tok
    row_tok = jnp.arange(rows)[:, None] % n_tok

    def mask_for(n_keys, causal):
        col = jnp.arange(n_keys * SUBLANES)[None, :]
        ok = (col % SUBLANES) == row_head
        if causal:
            ok = ok & ((col // SUBLANES) <= row_tok)
        return jnp.where(ok, 0.0, -jnp.inf).astype(F32)

    q_rows = q.reshape(n_b, n_tok, n_half, SUBLANES, d).transpose(0, 2, 3, 1, 4)
    q_rows = q_rows.reshape(n_b, n_half, rows, d).astype(BF16)
    cache4 = cache_kv.reshape(n_pool, page, 2 * n_heads, d)
    kvn = jnp.pad(kv_new.reshape(n_b, n_tok, 2 * n_heads, d),
                  ((0, 0), (0, t_pad - n_tok), (0, 0), (0, 0)))

    def page_pos(j, i):
        return n_pages - 1 - (j * pages_per_step + i)

    in_specs = [pl.BlockSpec((1, n_half, rows, d), lambda b, j, pt: (b, 0, 0, 0)),
                pl.BlockSpec((1, t_pad, 2 * n_heads, d), lambda b, j, pt: (b, 0, 0, 0)),
                pl.BlockSpec((1, n_half, 1, t_pad * SUBLANES), lambda b, j, pt: (b, 0, 0, 0)),
                pl.BlockSpec((rows, t_pad * SUBLANES), lambda b, j, pt: (0, 0)),
                pl.BlockSpec((rows, page * SUBLANES), lambda b, j, pt: (0, 0))]
    in_specs += [pl.BlockSpec((1, page, 2 * n_heads, d),
                              functools.partial(lambda b, j, pt, i: (pt[b, page_pos(j, i)], 0, 0, 0), i=i))
                 for i in range(pages_per_step)]
    in_specs += [pl.BlockSpec((1, 1, n_half, 1, page * SUBLANES),
                              functools.partial(lambda b, j, pt, i: (b, page_pos(j, i), 0, 0, 0), i=i))
                 for i in range(pages_per_step)]
    grid_spec = pltpu.PrefetchScalarGridSpec(
        num_scalar_prefetch=1,
        grid=(n_b, n_steps),
        in_specs=in_specs,
        out_specs=pl.BlockSpec((1, n_half, rows, d), lambda b, j, pt: (b, 0, 0, 0)),
        scratch_shapes=[pltpu.VMEM((n_half, rows, 1), F32), pltpu.VMEM((n_half, rows, 1), F32),
                        pltpu.VMEM((n_half, rows, d), F32)])
    o = pl.pallas_call(
        functools.partial(_paged_attn_kernel, n_heads=n_heads, pages_per_step=pages_per_step),
        grid_spec=grid_spec,
        out_shape=jax.ShapeDtypeStruct((n_b, n_half, rows, d), F32),
        compiler_params=_params("parallel", "arbitrary"),
        name=name,
    )(page_table, q_rows, kvn, bias_new, mask_for(t_pad, True), mask_for(page, False),
      *([cache4] * pages_per_step), *([bias_pages] * pages_per_step))
    o = o.reshape(n_b, n_half, SUBLANES, n_tok, d).transpose(0, 3, 1, 2, 4)
    return o.reshape(n_b, n_tok, hd)


def _prep_weights(p):
    d_model = p["a_w_in"].shape[1]
    w_in = p["a_w_in"][0]
    n_in = w_in.shape[1]
    d_inner = p["a_w_out"].shape[1]
    n_dt = n_in - (2 * d_inner + 2 * SSM_GROUPS * SSM_STATE)
    w_in_f32 = jnp.pad(w_in, ((0, 0), (0, LANES - n_dt)))
    w_in = w_in_f32.astype(BF16)
    hd = p["b_w_q"].shape[2]
    return dict(
        w_in=w_in, w_in_f32=w_in_f32, w_out=p["a_w_out"][0].astype(BF16),
        w_kv=p["w_kv"][:, :2 * hd].astype(BF16), w_f=p["w_kv"][:, 2 * hd:],
        w_q=p["b_w_q"][0].astype(BF16), w_o=p["b_w_o"][0].astype(BF16),
        m_gate=p["m_w_gate"].astype(BF16), m_up=p["m_w_up"].astype(BF16),
        m_down=p["m_w_down"].astype(BF16), d_model=d_model, d_inner=d_inner, hd=hd)


def _moe_layer(x, p, w, layer, tm):
    gates, hn = moe_router(x, p["norm_ffn"][layer], p["m_w_group"][layer], p["m_b_group"][layer],
                           p["m_w_expert"][layer], p["m_b_expert"][layer], tm=tm,
                           name=f"router{layer}")
    return moe_ffn(hn, gates, x, w["m_gate"][layer], w["m_up"][layer], w["m_down"][layer], tm=tm,
                   name=f"moe{layer}")


def _trunk(x, conv_state, ssm_state, p, w, *, batch, seq, tm, attend):
    d_inner, hd = w["d_inner"], w["hd"]
    n_heads = hd // ATT_HEAD_DIM
    prompt = isinstance(attend, str)
    precise = not prompt
    n_in = w["w_in"].shape[1]
    if precise:
        tn_in = n_in // 27 if n_in % (27 * LANES) == 0 else LANES
    else:
        tn_in = n_in // 9 if n_in % (9 * LANES) == 0 else LANES
    proj = norm_matmul(x, p["norm_mix"][0], w["w_in_f32"] if precise else w["w_in"],
                       tm=tm, tn=tn_in, precise=precise, name="in_proj")
    yn, ssm_new, conv_new = ssd_mixer(
        proj, conv_state, ssm_state, p["a_conv_w"][0], p["a_conv_b"][0], p["a_dt_bias"][0],
        p["a_a_log"][0], p["a_d_skip"][0], p["a_norm"][0],
        batch=batch, seq=seq, chunk=min(128, seq), d_inner=d_inner, precise=precise)
    x = matmul_res(yn, p["a_w_out"][0] if precise else w["w_out"], x, tm=tm,
                   tn=256 if precise else 512, precise=precise, name="out_proj")
    x = _moe_layer(x, p, w, 0, tm)

    kv_out = norm_matmul(x, p["kv_norm"], w["w_kv"], tm=tm, tn=512, head_g=p["k_norm"],
                         n_normed_cols=hd, bf16_copy=prompt, name="kv_proj")
    logf_out = logf_proj(x, p["kv_norm"], w["w_f"], p["b_forget"], tm=tm, with_bias_cols=prompt)
    logf = logf_out[0]
    q = norm_matmul(x, p["norm_mix"][1], w["w_q"], tm=tm, tn=512, head_g=p["b_q_norm"][0],
                    n_normed_cols=hd, out_scale=ATT_HEAD_DIM ** -0.5 * LOG2E,
                    out_dtype=BF16 if prompt else F32, name="q_proj")
    if prompt:
        kv, kv_bf = kv_out
        o = flash_attention(q, kv_bf, logf_out[1], n_heads=n_heads, tq=512, heads_per_step=2)
    else:
        kv = kv_out
        cache_kv, cache_logf, page_table = attend
        o = paged_attention(q.reshape(batch, seq, hd), kv.reshape(batch, seq, 2 * hd),
                            logf[:, :n_heads].reshape(batch, seq, n_heads), cache_kv, cache_logf,
                            page_table, pages_per_step=4).reshape(batch * seq, hd)
    x = matmul_res(o, w["w_o"], x, tm=tm, tn=512, name="o_proj")
    x = _moe_layer(x, p, w, 1, tm)
    return x, ssm_new, conv_new, kv, logf[:, :n_heads]


def kernel(x_prompt, x_sample, state_ssm, state_conv, cache_kv, cache_logf, page_table, norm_mix, norm_ffn, a_w_in, a_conv_w, a_conv_b, a_dt_bias, a_a_log, a_d_skip, a_norm, a_w_out, kv_norm, w_kv, b_forget, k_norm, b_w_q, b_q_norm, b_w_o, m_w_group, m_b_group, m_w_expert, m_b_expert, m_w_gate, m_w_up, m_w_down):
    p = dict(norm_mix=norm_mix, norm_ffn=norm_ffn, a_w_in=a_w_in, a_conv_w=a_conv_w,
             a_conv_b=a_conv_b, a_dt_bias=a_dt_bias, a_a_log=a_a_log, a_d_skip=a_d_skip,
             a_norm=a_norm, a_w_out=a_w_out, kv_norm=kv_norm, w_kv=w_kv, b_forget=b_forget,
             k_norm=k_norm, b_w_q=b_w_q, b_q_norm=b_q_norm, b_w_o=b_w_o, m_w_group=m_w_group,
             m_b_group=m_b_group, m_w_expert=m_w_expert, m_b_expert=m_b_expert,
             m_w_gate=m_w_gate, m_w_up=m_w_up, m_w_down=m_w_down)
    assert a_w_in.shape[0] == 1 and b_w_q.shape[0] == 1, "one SSD layer then one attention layer"
    w = _prep_weights(p)
    d_model, d_inner, hd = w["d_model"], w["d_inner"], w["hd"]
    n_heads = hd // ATT_HEAD_DIM
    conv_dim = state_conv.shape[-1]
    ssm_heads = state_ssm.shape[2]

    b_p, seq_p, _ = x_prompt.shape
    assert b_p == 1, "the prompt path carries one cumulative forget sum over its rows"
    conv0 = jnp.zeros((b_p, CONV_WIDTH - 1, conv_dim), F32)
    ssm0 = jnp.zeros((b_p, ssm_heads, SSM_HEAD_DIM, SSM_STATE), F32)
    y_p, ssm_p, conv_p, kv_p, logf_p = _trunk(
        x_prompt.reshape(b_p * seq_p, d_model), conv0, ssm0, p, w,
        batch=b_p, seq=seq_p, tm=512, attend="prompt")

    b_s, seq_s, _ = x_sample.shape
    y_s, ssm_s, conv_s, kv_s, logf_s = _trunk(
        x_sample.reshape(b_s * seq_s, d_model), state_conv[0], state_ssm[0], p, w,
        batch=b_s, seq=seq_s, tm=b_s * seq_s, attend=(cache_kv, cache_logf, page_table))

    return (y_p.reshape(b_p, seq_p, d_model), y_s.reshape(b_s, seq_s, d_model),
            ssm_p[None], conv_p[None],
            kv_p.reshape(b_p, seq_p, 2, n_heads, ATT_HEAD_DIM), logf_p.reshape(b_p, seq_p, n_heads),
            ssm_s[None], conv_s[None],
            kv_s.reshape(b_s, seq_s, 2, n_heads, ATT_HEAD_DIM), logf_s.reshape(b_s, seq_s, n_heads))
```

```python
import functools

import jax
import jax.numpy as jnp
from jax import lax
from jax.experimental import pallas as pl
from jax.experimental.pallas import tpu as pltpu

F32 = jnp.float32
BF16 = jnp.bfloat16

LANES = 128
SUBLANES = 8
VMEM_LIMIT = 56 * 1024 * 1024

RMS_EPS = 1e-5
LOG2E = 1.4426950408889634
SSM_HEAD_DIM = 64
SSM_GROUPS = 8
SSM_STATE = 128
CONV_WIDTH = 4
ATT_HEAD_DIM = 128
N_EXPERT_GROUPS = 4
EXPERTS_PER_GROUP = 4
N_EXPERTS = N_EXPERT_GROUPS * EXPERTS_PER_GROUP


def _params(*sem):
    return pltpu.CompilerParams(dimension_semantics=sem, vmem_limit_bytes=VMEM_LIMIT)


def _dot(a, b):
    return jnp.dot(a, b, preferred_element_type=F32)


def _dot_nt(a, b):
    return lax.dot_general(a, b, (((1,), (1,)), ((), ())), preferred_element_type=F32)


def _dot_tn(a, b):
    return lax.dot_general(a, b, (((0,), (0,)), ((), ())), preferred_element_type=F32)


def _split_bf16(v, terms):
    out = []
    r = v
    for _ in range(terms):
        t = r.astype(BF16)
        out.append(t)
        r = r - t.astype(F32)
    return out


def _mm(a, b, dot, precise):
    if not precise:
        return dot(a.astype(BF16), b.astype(BF16))
    a_hi, a_lo = _split_bf16(a, 2)
    b_hi, b_lo = _split_bf16(b, 2)
    return dot(a_hi, b_hi) + (dot(a_lo, b_hi) + dot(a_hi, b_lo))


def _silu(x):
    return x / (1.0 + jnp.exp(-x))


def _softplus(x):
    return jnp.maximum(x, 0.0) + jnp.log1p(jnp.exp(-jnp.abs(x)))


def _rms_rows(x, g):
    ms = jnp.mean(x * x, axis=-1, keepdims=True)
    return x * lax.rsqrt(ms + RMS_EPS) * g


def _norm_matmul_kernel(*refs, n_normed_tiles, has_head_norm, out_scale, bf16_copy, precise):
    refs = list(refs)
    h_scr = refs.pop()
    o2_ref = refs.pop() if bf16_copy else None
    o_ref = refs.pop()
    hg_ref = refs.pop() if has_head_norm else None
    x_ref, g_ref, w_ref = refs
    j = pl.program_id(1)

    @pl.when(j == 0)
    def _():
        h_scr[...] = _rms_rows(x_ref[...], g_ref[...]).astype(h_scr.dtype)

    acc = _mm(h_scr[...], w_ref[...], _dot, precise)
    tn = acc.shape[1]

    def head_normed(a):
        parts = []
        for c in range(tn // ATT_HEAD_DIM):
            blk = a[:, c * ATT_HEAD_DIM:(c + 1) * ATT_HEAD_DIM]
            parts.append(_rms_rows(blk, hg_ref[...]))
        return jnp.concatenate(parts, axis=1)

    def emit(a):
        if out_scale != 1.0:
            a = a * out_scale
        o_ref[...] = a.astype(o_ref.dtype)
        if bf16_copy:
            o2_ref[...] = a.astype(BF16)

    if not has_head_norm:
        emit(acc)
    else:
        @pl.when(j < n_normed_tiles)
        def _():
            emit(head_normed(acc))

        @pl.when(j >= n_normed_tiles)
        def _():
            emit(acc)


def norm_matmul(x, g, w, *, tm, tn, head_g=None, n_normed_cols=0, out_scale=1.0, out_dtype=F32,
                bf16_copy=False, precise=False, name="norm_matmul"):
    m, k = x.shape
    n = w.shape[1]
    assert m % tm == 0 and n % tn == 0 and n_normed_cols % tn == 0
    has_hn = head_g is not None
    in_specs = [pl.BlockSpec((tm, k), lambda i, j: (i, 0)),
                pl.BlockSpec((1, k), lambda i, j: (0, 0)),
                pl.BlockSpec((k, tn), lambda i, j: (0, j))]
    args = [x, g.reshape(1, k), w]
    if has_hn:
        in_specs.append(pl.BlockSpec((1, ATT_HEAD_DIM), lambda i, j: (0, 0)))
        args.append(head_g.reshape(1, ATT_HEAD_DIM))
    out_specs = [pl.BlockSpec((tm, tn), lambda i, j: (i, j))] * 2
    out_shape = [jax.ShapeDtypeStruct((m, n), out_dtype), jax.ShapeDtypeStruct((m, n), BF16)]
    return pl.pallas_call(
        functools.partial(_norm_matmul_kernel, n_normed_tiles=n_normed_cols // tn,
                          has_head_norm=has_hn, out_scale=out_scale, bf16_copy=bf16_copy,
                          precise=precise),
        grid=(m // tm, n // tn),
        in_specs=in_specs,
        out_specs=out_specs if bf16_copy else out_specs[0],
        out_shape=out_shape if bf16_copy else out_shape[0],
        scratch_shapes=[pltpu.VMEM((tm, k), F32 if precise else BF16)],
        compiler_params=_params("parallel", "arbitrary"),
        name=name,
    )(*args)


def _matmul_res_kernel(a_ref, w_ref, r_ref, o_ref, *, precise):
    o_ref[...] = r_ref[...] + _mm(a_ref[...], w_ref[...], _dot, precise)


def matmul_res(a, w, res, *, tm, tn, precise=False, name="matmul_res"):
    m, k = a.shape
    n = w.shape[1]
    assert m % tm == 0 and n % tn == 0
    return pl.pallas_call(
        functools.partial(_matmul_res_kernel, precise=precise),
        grid=(m // tm, n // tn),
        in_specs=[pl.BlockSpec((tm, k), lambda i, j: (i, 0)),
                  pl.BlockSpec((k, tn), lambda i, j: (0, j)),
                  pl.BlockSpec((tm, tn), lambda i, j: (i, j))],
        out_specs=pl.BlockSpec((tm, tn), lambda i, j: (i, j)),
        out_shape=jax.ShapeDtypeStruct((m, n), F32),
        compiler_params=_params("parallel", "arbitrary"),
        name=name,
    )(a, w, res)


def _ssd_kernel(z_ref, xs_ref, b_ref, c_ref, dt_ref, conv0_ref, ssm0_ref, cw_ref, cbias_ref,
                dtb_ref, alog_ref, dskip_ref, gn_ref, expand_ref, tril_ref, eye_ref,
                yn_ref, ssm_out_ref, conv_out_ref,
                ext_scr, act_scr, y_scr, h_scr, *, chunk, n_heads, d_inner, gn_cols, precise):
    q = chunk
    c = pl.program_id(1)
    nc = pl.num_programs(1)
    pad = SUBLANES
    heads_per_group = n_heads // SSM_GROUPS
    gcols = heads_per_group * SSM_HEAD_DIM

    @pl.when(c == 0)
    def _():
        ext_scr[0:pad, :] = conv0_ref[0]
        h_scr[...] = ssm0_ref[0]

    ext_scr[pad:pad + q, 0:d_inner] = xs_ref[...]
    ext_scr[pad:pad + q, d_inner:d_inner + gn_cols] = b_ref[...]
    ext_scr[pad:pad + q, d_inner + gn_cols:] = c_ref[...]

    acc = jnp.broadcast_to(cbias_ref[...], (q, ext_scr.shape[1]))
    for tap in range(CONV_WIDTH):
        start = pad - (CONV_WIDTH - 1) + tap
        acc = acc + ext_scr[start:start + q, :] * cw_ref[tap:tap + 1, :]
    act_scr[...] = _silu(acc)

    tail = ext_scr[q:q + pad, :]
    conv_out_ref[0] = tail
    ext_scr[0:pad, :] = tail

    dt = _softplus(dt_ref[...] + dtb_ref[...])
    a = dt * (-jnp.exp(alog_ref[...]))
    tril = tril_ref[...]
    eye = eye_ref[...]
    a_cum = sum(_dot(tril, t) for t in _split_bf16(a, 3))
    a_cum_t = sum(_dot_nt(eye, t) for t in _split_bf16(a_cum, 3))
    dt_t = sum(_dot_nt(eye, t) for t in _split_bf16(dt, 3))
    a_last = a_cum[q - 1:q, :]
    w_state = dt * jnp.exp(a_last - a_cum)
    e_cum = jnp.exp(a_cum)
    expand = expand_ref[...]
    w_state_x = sum(_dot(t, expand) for t in _split_bf16(w_state, 2))
    e_cum_x = sum(_dot(t, expand) for t in _split_bf16(e_cum, 2))

    row_i = lax.broadcasted_iota(jnp.int32, (q, q), 0)
    col_i = lax.broadcasted_iota(jnp.int32, (q, q), 1)
    causal = col_i <= row_i
    lane = lax.broadcasted_iota(jnp.int32, (q, LANES), 1)
    lo_half = lane < SSM_HEAD_DIM

    operand = (lambda v: v) if precise else (lambda v: v.astype(BF16))
    for g in range(SSM_GROUPS):
        bg = operand(act_scr[:, d_inner + g * SSM_STATE:d_inner + (g + 1) * SSM_STATE])
        cg = operand(act_scr[:, d_inner + gn_cols + g * SSM_STATE:
                             d_inner + gn_cols + (g + 1) * SSM_STATE])
        cb = _mm(cg, bg, _dot_nt, precise)
        for j in range(gcols // LANES):
            c0 = g * gcols + j * LANES
            xs_blk = act_scr[:, c0:c0 + LANES]
            y_pair = None
            for half in range(2):
                h = (c0 // SSM_HEAD_DIM) + half
                seg = a_cum[:, h:h + 1] - a_cum_t[h:h + 1, :]
                m = cb * jnp.exp(jnp.where(causal, seg, -jnp.inf)) * dt_t[h:h + 1, :]
                keep = lo_half if half == 0 else jnp.logical_not(lo_half)
                part = _mm(m, jnp.where(keep, xs_blk, 0.0), _dot, precise)
                y_pair = part if y_pair is None else y_pair + part
            y_scr[:, c0:c0 + LANES] = y_pair
        gs = slice(g * gcols, (g + 1) * gcols)
        h_g = h_scr[:, gs]
        y_off = _mm(cg, h_g, _dot, precise) * e_cum_x[:, gs]
        y_scr[:, gs] = y_scr[:, gs] + y_off
        xw = act_scr[:, gs] * w_state_x[:, gs]
        h_scr[:, gs] = h_g * e_cum_x[q - 1:q, gs] + _mm(bg, xw, _dot_tn, precise)

    y = y_scr[...] + act_scr[:, 0:d_inner] * dskip_ref[...]
    y = y * _silu(z_ref[...])
    gw = d_inner // SSM_GROUPS
    parts = []
    for g in range(SSM_GROUPS):
        parts.append(_rms_rows(y[:, g * gw:(g + 1) * gw], gn_ref[:, g * gw:(g + 1) * gw]))
    yn_ref[...] = jnp.concatenate(parts, axis=1).astype(yn_ref.dtype)

    @pl.when(c == nc - 1)
    def _():
        ssm_out_ref[0] = h_scr[...]


def ssd_mixer(proj, conv_state, ssm_state, conv_w, conv_b, dt_bias, a_log, d_skip, gn,
              *, batch, seq, chunk, d_inner, precise=False, name="ssd"):
    n_heads = d_inner // SSM_HEAD_DIM
    gn_cols = SSM_GROUPS * SSM_STATE
    conv_dim = d_inner + 2 * gn_cols
    nc = seq // chunk
    assert seq % chunk == 0 and n_heads <= LANES
    q = chunk
    pad = SUBLANES

    conv0 = jnp.concatenate(
        [jnp.zeros((batch, pad - (CONV_WIDTH - 1), conv_dim), F32), conv_state], axis=1)
    ssm0 = jnp.transpose(ssm_state.reshape(batch, d_inner, SSM_STATE), (0, 2, 1))

    def lane_pad(v):
        return jnp.pad(v, (0, LANES - v.shape[0])).reshape(1, LANES)

    head_of_col = jnp.arange(d_inner) // SSM_HEAD_DIM
    expand = (jnp.arange(LANES)[:, None] == head_of_col[None, :]).astype(BF16)
    tril = jnp.tril(jnp.ones((q, q), BF16))
    eye = jnp.eye(LANES, dtype=BF16)
    dskip_cols = jnp.repeat(d_skip, SSM_HEAD_DIM).reshape(1, d_inner)

    zb = d_inner // d_inner
    row = lambda b, c: b * nc + c
    const2 = lambda b, c: (0, 0)
    out_dtype = BF16 if (q % 16 == 0 and not precise) else F32
    yn, ssm_t, conv_tail = pl.pallas_call(
        functools.partial(_ssd_kernel, chunk=q, n_heads=n_heads, d_inner=d_inner, gn_cols=gn_cols,
                          precise=precise),
        grid=(batch, nc),
        in_specs=[
            pl.BlockSpec((q, d_inner), lambda b, c: (row(b, c), 0)),
            pl.BlockSpec((q, d_inner), lambda b, c: (row(b, c), zb)),
            pl.BlockSpec((q, gn_cols), lambda b, c: (row(b, c), 2 * d_inner // gn_cols)),
            pl.BlockSpec((q, gn_cols), lambda b, c: (row(b, c), 2 * d_inner // gn_cols + 1)),
            pl.BlockSpec((q, LANES), lambda b, c: (row(b, c), (2 * d_inner + 2 * gn_cols) // LANES)),
            pl.BlockSpec((1, pad, conv_dim), lambda b, c: (b, 0, 0)),
            pl.BlockSpec((1, SSM_STATE, d_inner), lambda b, c: (b, 0, 0)),
            pl.BlockSpec((CONV_WIDTH, conv_dim), const2),
            pl.BlockSpec((1, conv_dim), const2),
            pl.BlockSpec((1, LANES), const2),
            pl.BlockSpec((1, LANES), const2),
            pl.BlockSpec((1, d_inner), const2),
            pl.BlockSpec((1, d_inner), const2),
            pl.BlockSpec((LANES, d_inner), const2),
            pl.BlockSpec((q, q), const2),
            pl.BlockSpec((LANES, LANES), const2),
        ],
        out_specs=[
            pl.BlockSpec((q, d_inner), lambda b, c: (row(b, c), 0)),
            pl.BlockSpec((1, SSM_STATE, d_inner), lambda b, c: (b, 0, 0)),
            pl.BlockSpec((1, pad, conv_dim), lambda b, c: (b, 0, 0)),
        ],
        out_shape=[
            jax.ShapeDtypeStruct((batch * seq, d_inner), out_dtype),
            jax.ShapeDtypeStruct((batch, SSM_STATE, d_inner), F32),
            jax.ShapeDtypeStruct((batch, pad, conv_dim), F32),
        ],
        scratch_shapes=[
            pltpu.VMEM((q + pad, conv_dim), F32),
            pltpu.VMEM((q, conv_dim), F32),
            pltpu.VMEM((q, d_inner), F32),
            pltpu.VMEM((SSM_STATE, d_inner), F32),
        ],
        compiler_params=_params("parallel", "arbitrary"),
        name=name,
    )(proj, proj, proj, proj, proj, conv0, ssm0, conv_w, conv_b.reshape(1, conv_dim),
      lane_pad(dt_bias), lane_pad(a_log), dskip_cols, gn.reshape(1, d_inner), expand, tril, eye)
    ssm_new = jnp.transpose(ssm_t, (0, 2, 1)).reshape(batch, n_heads, SSM_HEAD_DIM, SSM_STATE)
    return yn, ssm_new, conv_tail[:, pad - (CONV_WIDTH - 1):]


def _router_kernel(x_ref, g_ref, whi_ref, wlo_ref, bias_ref, gates_ref, hn_ref):
    h = _rms_rows(x_ref[...], g_ref[...])
    hn_ref[...] = h.astype(BF16)
    h_hi, h_lo = _split_bf16(h, 2)
    logits = (_dot(h_hi, whi_ref[...]) + _dot(h_lo, whi_ref[...]) + _dot(h_hi, wlo_ref[...])
              + bias_ref[...])
    tm = logits.shape[0]
    lane = lax.broadcasted_iota(jnp.int32, (tm, LANES), 1)
    neg = -jnp.inf
    is_group = lane < N_EXPERT_GROUPS
    gl = jnp.where(is_group, logits, neg)
    g_max = jnp.max(gl, axis=-1, keepdims=True)
    g_idx = jnp.min(jnp.where(gl == g_max, lane, LANES), axis=-1, keepdims=True)
    g_w = 1.0 / jnp.sum(jnp.where(is_group, jnp.exp(gl - g_max), 0.0), axis=-1, keepdims=True)
    first = N_EXPERT_GROUPS + g_idx * EXPERTS_PER_GROUP
    in_group = (lane >= first) & (lane < first + EXPERTS_PER_GROUP)
    el = jnp.where(in_group, logits, neg)
    v1 = jnp.max(el, axis=-1, keepdims=True)
    i1 = jnp.min(jnp.where(el == v1, lane, LANES), axis=-1, keepdims=True)
    el2 = jnp.where(lane == i1, neg, el)
    v2 = jnp.max(el2, axis=-1, keepdims=True)
    i2 = jnp.min(jnp.where(el2 == v2, lane, LANES), axis=-1, keepdims=True)
    e2 = jnp.exp(v2 - v1)
    w1 = g_w / (1.0 + e2)
    w2 = g_w * e2 / (1.0 + e2)
    gates = jnp.where(lane == i1, w1, 0.0) + jnp.where(lane == i2, w2, 0.0)
    gates_ref[...] = pltpu.roll(gates, LANES - N_EXPERT_GROUPS, 1)


def moe_router(x, g, w_group, b_group, w_expert, b_expert, *, tm, name="router"):
    m, k = x.shape
    w = jnp.concatenate([w_group, w_expert.reshape(k, N_EXPERTS)], axis=1)
    w = jnp.pad(w, ((0, 0), (0, LANES - w.shape[1])))
    w_hi = w.astype(BF16)
    w_lo = (w - w_hi.astype(F32)).astype(BF16)
    bias = jnp.pad(jnp.concatenate([b_group, b_expert.reshape(N_EXPERTS)]),
                   (0, LANES - N_EXPERT_GROUPS - N_EXPERTS)).reshape(1, LANES)
    return pl.pallas_call(
        _router_kernel,
        grid=(m // tm,),
        in_specs=[pl.BlockSpec((tm, k), lambda i: (i, 0)),
                  pl.BlockSpec((1, k), lambda i: (0, 0)),
                  pl.BlockSpec((k, LANES), lambda i: (0, 0)),
                  pl.BlockSpec((k, LANES), lambda i: (0, 0)),
                  pl.BlockSpec((1, LANES), lambda i: (0, 0))],
        out_specs=[pl.BlockSpec((tm, LANES), lambda i: (i, 0)),
                   pl.BlockSpec((tm, k), lambda i: (i, 0))],
        out_shape=[jax.ShapeDtypeStruct((m, LANES), F32),
                   jax.ShapeDtypeStruct((m, k), BF16)],
        compiler_params=_params("parallel"),
        name=name,
    )(x, g.reshape(1, k), w_hi, w_lo, bias)


def _moe_kernel(hn_ref, gates_ref, x_ref, wg_ref, wu_ref, wd_ref, o_ref):
    e = pl.program_id(1)

    @pl.when(e == 0)
    def _():
        o_ref[...] = x_ref[...]

    h = hn_ref[...]
    lane = lax.broadcasted_iota(jnp.int32, gates_ref.shape, 1)
    gate = jnp.sum(jnp.where(lane == e, gates_ref[...], 0.0), axis=-1, keepdims=True)
    hidden = _silu(_dot(h, wg_ref[0])) * _dot(h, wu_ref[0]) * gate
    o_ref[...] += _dot(hidden.astype(BF16), wd_ref[0])


def moe_ffn(hn, gates, x, w_gate, w_up, w_down, *, tm, name="moe"):
    m, k = hn.shape
    n_e, _, f = w_gate.shape
    return pl.pallas_call(
        _moe_kernel,
        grid=(m // tm, n_e),
        in_specs=[pl.BlockSpec((tm, k), lambda i, e: (i, 0)),
                  pl.BlockSpec((tm, LANES), lambda i, e: (i, 0)),
                  pl.BlockSpec((tm, k), lambda i, e: (i, 0)),
                  pl.BlockSpec((1, k, f), lambda i, e: (e, 0, 0)),
                  pl.BlockSpec((1, k, f), lambda i, e: (e, 0, 0)),
                  pl.BlockSpec((1, f, k), lambda i, e: (e, 0, 0))],
        out_specs=pl.BlockSpec((tm, k), lambda i, e: (i, 0)),
        out_shape=jax.ShapeDtypeStruct((m, k), F32),
        compiler_params=_params("parallel", "arbitrary"),
        name=name,
    )(hn, gates, x, w_gate, w_up, w_down)


def _logf_kernel(x_ref, g_ref, w_ref, b_ref, *rest, n_heads, with_bias_cols):
    h = _rms_rows(x_ref[...], g_ref[...]).astype(BF16)
    zf = _dot(h, w_ref[...]) + b_ref[...]
    logf = -_softplus(-zf)
    if not with_bias_cols:
        (logf_ref,) = rest
        logf_ref[...] = logf
        return
    tril_ref, logf_ref, bias_ref, carry_scr = rest
    logf_ref[...] = logf

    @pl.when(pl.program_id(0) == 0)
    def _():
        carry_scr[...] = jnp.zeros_like(carry_scr)

    cum = sum(_dot(tril_ref[...], t) for t in _split_bf16(logf, 3)) + carry_scr[...]
    tm = cum.shape[0]
    carry_scr[...] = cum[tm - 1:tm, :]
    lane = lax.broadcasted_iota(jnp.int32, (tm, LANES), 1)
    for hd in range(n_heads):
        t0, t1, t2 = [t.astype(F32) for t in _split_bf16(cum[:, hd:hd + 1] * (-LOG2E), 3)]
        blk = jnp.where(lane == 0, t0, jnp.where(lane == 1, t1, jnp.where(lane == 2, t2, 0.0)))
        bias_ref[:, hd * LANES:(hd + 1) * LANES] = blk.astype(BF16)


def logf_proj(x, g, w_f, b_f, *, tm, with_bias_cols, name="logf"):
    m, k = x.shape
    n_h = w_f.shape[1]
    w = jnp.pad(w_f, ((0, 0), (0, LANES - n_h))).astype(BF16)
    b = jnp.pad(b_f, (0, LANES - n_h)).reshape(1, LANES)
    in_specs = [pl.BlockSpec((tm, k), lambda i: (i, 0)),
                pl.BlockSpec((1, k), lambda i: (0, 0)),
                pl.BlockSpec((k, LANES), lambda i: (0, 0)),
                pl.BlockSpec((1, LANES), lambda i: (0, 0))]
    args = [x, g.reshape(1, k), w, b]
    out_specs = [pl.BlockSpec((tm, LANES), lambda i: (i, 0))]
    out_shape = [jax.ShapeDtypeStruct((m, LANES), F32)]
    scratch = []
    if with_bias_cols:
        in_specs.append(pl.BlockSpec((tm, tm), lambda i: (0, 0)))
        args.append(jnp.tril(jnp.ones((tm, tm), BF16)))
        out_specs.append(pl.BlockSpec((tm, n_h * LANES), lambda i: (i, 0)))
        out_shape.append(jax.ShapeDtypeStruct((m, n_h * LANES), BF16))
        scratch.append(pltpu.VMEM((1, LANES), F32))
    return pl.pallas_call(
        functools.partial(_logf_kernel, n_heads=n_h, with_bias_cols=with_bias_cols),
        grid=(m // tm,),
        in_specs=in_specs,
        out_specs=out_specs,
        out_shape=out_shape,
        scratch_shapes=scratch,
        compiler_params=_params("arbitrary"),
        name=name,
    )(*args)


def _flash_kernel(q_ref, k_ref, v_ref, bias_ref, o_ref, vt_scr, m_scr, l_scr, acc_scr, *,
                  tq, heads_per_step):
    qi = pl.program_id(1)
    d = ATT_HEAD_DIM
    n_blocks = k_ref.shape[0] // tq

    @pl.when(qi == 0)
    def _():
        for hh in range(heads_per_step):
            for c in range(n_blocks):
                blk = v_ref[c * tq:(c + 1) * tq, hh * d:(hh + 1) * d].astype(F32)
                vt_scr[hh, c] = blk.T.astype(BF16)

    lane = lax.broadcasted_iota(jnp.int32, (tq, d), 1)
    ones_cols = jnp.where(lane < 3, 1.0, 0.0).astype(BF16)
    q_aug = [jnp.concatenate([q_ref[:, hh * d:(hh + 1) * d], ones_cols], axis=1)
             for hh in range(heads_per_step)]

    def block(ki, n_sub, masked, first):
        for hh in range(heads_per_step):
            scores = []
            for u in range(n_sub):
                start = pl.multiple_of((ki + u) * tq, tq)
                k_aug = jnp.concatenate([k_ref[pl.ds(start, tq), hh * d:(hh + 1) * d],
                                         bias_ref[pl.ds(start, tq), hh * d:(hh + 1) * d]], axis=1)
                s = _dot_nt(k_aug, q_aug[hh])
                if masked:
                    key_i = lax.broadcasted_iota(jnp.int32, (tq, tq), 0)
                    qry_i = lax.broadcasted_iota(jnp.int32, (tq, tq), 1)
                    s = jnp.where(key_i <= qry_i, s, -jnp.inf)
                scores.append(s)
            s_max = functools.reduce(jnp.maximum,
                                     [jnp.max(s, axis=0, keepdims=True) for s in scores])
            if first:
                m_new = s_max
            else:
                m_prev = m_scr[hh]
                m_new = jnp.maximum(m_prev, s_max)
                alpha = jnp.exp2(m_prev - m_new)
            probs = [jnp.exp2(s - m_new) for s in scores]
            l_new = sum(jnp.sum(p, axis=0, keepdims=True) for p in probs)
            pv = sum(_dot(vt_scr[hh, ki + u], p.astype(BF16)) for u, p in enumerate(probs))
            if first:
                l_scr[hh] = l_new
                acc_scr[hh] = pv
            else:
                l_scr[hh] = l_scr[hh] * alpha + l_new
                acc_scr[hh] = acc_scr[hh] * alpha + pv
            m_scr[hh] = m_new

    def pair_body(kp, carry):
        block(2 * kp, 2, False, False)
        return carry

    block(qi, 1, True, True)
    lax.fori_loop(0, qi // 2, pair_body, 0)

    @pl.when(qi % 2 == 1)
    def _():
        block(qi - 1, 1, False, False)
    for hh in range(heads_per_step):
        out_t = acc_scr[hh] / l_scr[hh]
        o_ref[:, hh * d:(hh + 1) * d] = out_t.T.astype(o_ref.dtype)


def flash_attention(q, kv, bias_cols, *, n_heads, tq, heads_per_step, name="flash"):
    seq = q.shape[0]
    d = ATT_HEAD_DIM
    w = heads_per_step * d
    n_steps = n_heads // heads_per_step
    assert n_heads % heads_per_step == 0 and seq % tq == 0
    return pl.pallas_call(
        functools.partial(_flash_kernel, tq=tq, heads_per_step=heads_per_step),
        grid=(n_steps, seq // tq),
        in_specs=[pl.BlockSpec((tq, w), lambda h, i: (i, h)),
                  pl.BlockSpec((seq, w), lambda h, i: (0, h)),
                  pl.BlockSpec((seq, w), lambda h, i: (0, n_steps + h)),
                  pl.BlockSpec((seq, w), lambda h, i: (0, h))],
        out_specs=pl.BlockSpec((tq, w), lambda h, i: (i, h)),
        out_shape=jax.ShapeDtypeStruct((seq, n_heads * d), BF16),
        scratch_shapes=[pltpu.VMEM((heads_per_step, seq // tq, d, tq), BF16),
                        pltpu.VMEM((heads_per_step, 1, tq), F32),
                        pltpu.VMEM((heads_per_step, 1, tq), F32),
                        pltpu.VMEM((heads_per_step, d, tq), F32)],
        compiler_params=_params("parallel", "arbitrary"),
        name=name,
    )(q, kv, kv, bias_cols)


def _page_sums_kernel(lf_ref, upper_ref, ones_ref, inner_ref, total_ref):
    terms = _split_bf16(lf_ref[...], 3)
    inner_ref[...] = sum(_dot(upper_ref[...], t) for t in terms) * LOG2E
    total_ref[...] = sum(_dot(ones_ref[...], t) for t in terms) * LOG2E


def page_forget_sums(cache_logf, name="page_sums"):
    n_pool, page, n_heads = cache_logf.shape
    cols = n_pool * n_heads
    lf_t = jnp.transpose(cache_logf, (1, 0, 2)).reshape(page, cols)
    tn = max([t for t in range(LANES, 4096 + 1, LANES) if cols % t == 0], default=cols)
    return pl.pallas_call(
        _page_sums_kernel,
        grid=(cols // tn,),
        in_specs=[pl.BlockSpec((page, tn), lambda i: (0, i)),
                  pl.BlockSpec((page, page), lambda i: (0, 0)),
                  pl.BlockSpec((SUBLANES, page), lambda i: (0, 0))],
        out_specs=[pl.BlockSpec((page, tn), lambda i: (0, i)),
                   pl.BlockSpec((SUBLANES, tn), lambda i: (0, i))],
        out_shape=[jax.ShapeDtypeStruct((page, cols), F32),
                   jax.ShapeDtypeStruct((SUBLANES, cols), F32)],
        compiler_params=_params("parallel"),
        name=name,
    )(lf_t, jnp.triu(jnp.ones((page, page), BF16), k=1), jnp.ones((SUBLANES, page), BF16))


def _new_token_bias_kernel(lf_ref, tril_ref, o_ref):
    o_ref[...] = sum(_dot(tril_ref[...], t) for t in _split_bf16(lf_ref[...], 3)) * (-LOG2E)


def new_token_bias(logf_new, name="new_token_bias"):
    n_b, t_pad, n_heads = logf_new.shape
    rows = n_b * t_pad
    r = jnp.arange(rows)
    same_request = (r[:, None] // t_pad) == (r[None, :] // t_pad)
    tril = (same_request & (r[None, :] <= r[:, None])).astype(BF16)
    out = pl.pallas_call(
        _new_token_bias_kernel,
        out_shape=jax.ShapeDtypeStruct((rows, n_heads), F32),
        compiler_params=pltpu.CompilerParams(vmem_limit_bytes=VMEM_LIMIT),
        name=name,
    )(logf_new.reshape(rows, n_heads), tril)
    return out.reshape(n_b, t_pad, n_heads)


def _paged_attn_kernel(pt_ref, q_ref, kvn_ref, bn_ref, mask_new_ref, mask_page_ref, *rest,
                       n_heads, pages_per_step):
    kv_refs = rest[:pages_per_step]
    inner_refs = rest[pages_per_step:2 * pages_per_step]
    total_refs = rest[2 * pages_per_step:3 * pages_per_step]
    o_ref, m_scr, l_scr, acc_scr, carry_scr = rest[3 * pages_per_step:]
    del pt_ref
    j = pl.program_id(1)
    n_steps = pl.num_programs(1)
    n_half = n_heads // SUBLANES

    def rows_of(kv_view, first_head):
        n_keys = kv_view.shape[0]
        blk = kv_view[:, first_head:first_head + SUBLANES, :]
        return blk.reshape(n_keys * SUBLANES, ATT_HEAD_DIM).astype(BF16)

    def attend(kv_views, half, bias_rows, mask, first):
        lo = half * SUBLANES
        scores = [_dot_nt(q_ref[0, half], rows_of(view, lo)) + mask + bias
                  for view, bias in zip(kv_views, bias_rows)]
        s_max = functools.reduce(jnp.maximum, [jnp.max(s, axis=-1, keepdims=True) for s in scores])
        if first:
            m_new = s_max
        else:
            m_prev = m_scr[half]
            m_new = jnp.maximum(m_prev, s_max)
            alpha = jnp.exp2(m_prev - m_new)
        probs = [jnp.exp2(s - m_new) for s in scores]
        l_new = sum(jnp.sum(p, axis=-1, keepdims=True) for p in probs)
        pv = sum(_dot(p.astype(BF16), rows_of(view, n_heads + lo))
                 for p, view in zip(probs, kv_views))
        if first:
            l_scr[half] = l_new
            acc_scr[half] = pv
        else:
            l_scr[half] = l_scr[half] * alpha + l_new
            acc_scr[half] = acc_scr[half] * alpha + pv
        m_scr[half] = m_new

    @pl.when(j == 0)
    def _():
        for half in range(n_half):
            attend([kvn_ref.at[0]], half, [bn_ref[0, half]], mask_new_ref[...], True)
        carry_scr[...] = jnp.zeros_like(carry_scr)

    for half in range(n_half):
        carry = carry_scr[half]
        biases = []
        for inner, total in zip(inner_refs, total_refs):
            biases.append(inner[0, half] + carry)
            carry = carry + total[0, half]
        carry_scr[half] = carry
        attend([r.at[0] for r in kv_refs], half, biases, mask_page_ref[...], False)

    @pl.when(j == n_steps - 1)
    def _():
        for half in range(n_half):
            o_ref[0, half] = acc_scr[half] / l_scr[half]


def paged_attention(q, kv_new, logf_new, cache_kv, cache_logf, page_table, *, pages_per_step,
                    name="paged_attn"):
    n_b, n_tok, hd = q.shape
    d = ATT_HEAD_DIM
    n_heads = hd // d
    n_half = n_heads // SUBLANES
    n_pool, page = cache_logf.shape[0], cache_logf.shape[1]
    n_pages = page_table.shape[1]
    rows = SUBLANES * n_tok
    t_pad = -(-n_tok // 16) * 16
    assert n_pages % pages_per_step == 0 and n_heads % SUBLANES == 0 and rows % 16 == 0
    n_steps = n_pages // pages_per_step

    inner, total = page_forget_sums(cache_logf)
    inner_rows = inner.reshape(page, n_pool, n_half, SUBLANES).transpose(1, 2, 0, 3)
    inner_rows = inner_rows.reshape(n_pool, n_half, 1, page * SUBLANES)
    total_rows = jnp.broadcast_to(total[0].reshape(n_pool, n_half, 1, SUBLANES),
                                  (n_pool, n_half, page, SUBLANES))
    total_rows = total_rows.reshape(n_pool, n_half, 1, page * SUBLANES)
    neg_c_new = new_token_bias(jnp.pad(logf_new, ((0, 0), (0, t_pad - n_tok), (0, 0))))
    bias_new = neg_c_new.reshape(n_b, t_pad, n_half, SUBLANES).transpose(0, 2, 1, 3)
    bias_new = bias_new.reshape(n_b, n_half, 1, t_pad * SUBLANES)

    row_head = jnp.arange(rows)[:, None] // n_tok
    row_tok = jnp.arange(rows)[:, None] % n_tok

    def mask_for(n_keys, causal):
        col = jnp.arange(n_keys * SUBLANES)[None, :]
        ok = (col % SUBLANES) == row_head
        if causal:
            ok = ok & ((col // SUBLANES) <= row_tok)
        return jnp.where(ok, 0.0, -jnp.inf).astype(F32)

    q_rows = q.reshape(n_b, n_tok, n_half, SUBLANES, d).transpose(0, 2, 3, 1, 4)
    q_rows = q_rows.reshape(n_b, n_half, rows, d).astype(BF16)
    cache4 = cache_kv.reshape(n_pool, page, 2 * n_heads, d)
    kvn = jnp.pad(kv_new.reshape(n_b, n_tok, 2 * n_heads, d),
                  ((0, 0), (0, t_pad - n_tok), (0, 0), (0, 0)))

    def page_pos(j, i):
        return n_pages - 1 - (j * pages_per_step + i)

    in_specs = [pl.BlockSpec((1, n_half, rows, d), lambda b, j, pt: (b, 0, 0, 0)),
                pl.BlockSpec((1, t_pad, 2 * n_heads, d), lambda b, j, pt: (b, 0, 0, 0)),
                pl.BlockSpec((1, n_half, 1, t_pad * SUBLANES), lambda b, j, pt: (b, 0, 0, 0)),
                pl.BlockSpec((rows, t_pad * SUBLANES), lambda b, j, pt: (0, 0)),
                pl.BlockSpec((rows, page * SUBLANES), lambda b, j, pt: (0, 0))]
    in_specs += [pl.BlockSpec((1, page, 2 * n_heads, d),
                              functools.partial(lambda b, j, pt, i: (pt[b, page_pos(j, i)], 0, 0, 0), i=i))
                 for i in range(pages_per_step)]
    sums_specs = [pl.BlockSpec((1, n_half, 1, page * SUBLANES),
                               functools.partial(lambda b, j, pt, i: (pt[b, page_pos(j, i)], 0, 0, 0), i=i))
                  for i in range(pages_per_step)]
    in_specs += sums_specs + sums_specs
    grid_spec = pltpu.PrefetchScalarGridSpec(
        num_scalar_prefetch=1,
        grid=(n_b, n_steps),
        in_specs=in_specs,
        out_specs=pl.BlockSpec((1, n_half, rows, d), lambda b, j, pt: (b, 0, 0, 0)),
        scratch_shapes=[pltpu.VMEM((n_half, rows, 1), F32), pltpu.VMEM((n_half, rows, 1), F32),
                        pltpu.VMEM((n_half, rows, d), F32),
                        pltpu.VMEM((n_half, 1, page * SUBLANES), F32)])
    o = pl.pallas_call(
        functools.partial(_paged_attn_kernel, n_heads=n_heads, pages_per_step=pages_per_step),
        grid_spec=grid_spec,
        out_shape=jax.ShapeDtypeStruct((n_b, n_half, rows, d), F32),
        compiler_params=_params("parallel", "arbitrary"),
        name=name,
    )(page_table, q_rows, kvn, bias_new, mask_for(t_pad, True), mask_for(page, False),
      *([cache4] * pages_per_step), *([inner_rows] * pages_per_step),
      *([total_rows] * pages_per_step))
    o = o.reshape(n_b, n_half, SUBLANES, n_tok, d).transpose(0, 3, 1, 2, 4)
    return o.reshape(n_b, n_tok, hd)


def _prep_weights(p):
    d_model = p["a_w_in"].shape[1]
    w_in = p["a_w_in"][0]
    n_in = w_in.shape[1]
    d_inner = p["a_w_out"].shape[1]
    n_dt = n_in - (2 * d_inner + 2 * SSM_GROUPS * SSM_STATE)
    w_in_f32 = jnp.pad(w_in, ((0, 0), (0, LANES - n_dt)))
    w_in = w_in_f32.astype(BF16)
    hd = p["b_w_q"].shape[2]
    return dict(
        w_in=w_in, w_in_f32=w_in_f32, w_out=p["a_w_out"][0].astype(BF16),
        w_kv=p["w_kv"][:, :2 * hd].astype(BF16), w_f=p["w_kv"][:, 2 * hd:],
        w_q=p["b_w_q"][0].astype(BF16), w_o=p["b_w_o"][0].astype(BF16),
        m_gate=p["m_w_gate"].astype(BF16), m_up=p["m_w_up"].astype(BF16),
        m_down=p["m_w_down"].astype(BF16), d_model=d_model, d_inner=d_inner, hd=hd)


def _moe_layer(x, p, w, layer, tm):
    gates, hn = moe_router(x, p["norm_ffn"][layer], p["m_w_group"][layer], p["m_b_group"][layer],
                           p["m_w_expert"][layer], p["m_b_expert"][layer], tm=tm,
                           name=f"router{layer}")
    return moe_ffn(hn, gates, x, w["m_gate"][layer], w["m_up"][layer], w["m_down"][layer], tm=tm,
                   name=f"moe{layer}")


def _trunk(x, conv_state, ssm_state, p, w, *, batch, seq, tm, attend):
    d_inner, hd = w["d_inner"], w["hd"]
    n_heads = hd // ATT_HEAD_DIM
    prompt = isinstance(attend, str)
    precise = not prompt
    n_in = w["w_in"].shape[1]
    if precise:
        tn_in = n_in // 27 if n_in % (27 * LANES) == 0 else LANES
    else:
        tn_in = n_in // 9 if n_in % (9 * LANES) == 0 else LANES
    m_rows = batch * seq
    tm_mm = 2 * tm if (prompt and m_rows % (2 * tm) == 0) else tm
    tn_mm = 1024 if prompt else 512
    proj = norm_matmul(x, p["norm_mix"][0], w["w_in_f32"] if precise else w["w_in"],
                       tm=tm_mm, tn=tn_in, precise=precise, name="in_proj")
    yn, ssm_new, conv_new = ssd_mixer(
        proj, conv_state, ssm_state, p["a_conv_w"][0], p["a_conv_b"][0], p["a_dt_bias"][0],
        p["a_a_log"][0], p["a_d_skip"][0], p["a_norm"][0],
        batch=batch, seq=seq, chunk=min(128, seq), d_inner=d_inner, precise=precise)
    x = matmul_res(yn, p["a_w_out"][0] if precise else w["w_out"], x, tm=tm_mm,
                   tn=256 if precise else 512, precise=precise, name="out_proj")
    x = _moe_layer(x, p, w, 0, tm)

    kv_out = norm_matmul(x, p["kv_norm"], w["w_kv"], tm=tm_mm, tn=tn_mm, head_g=p["k_norm"],
                         n_normed_cols=hd, bf16_copy=prompt, name="kv_proj")
    logf_out = logf_proj(x, p["kv_norm"], w["w_f"], p["b_forget"], tm=tm, with_bias_cols=prompt)
    logf = logf_out[0]
    q = norm_matmul(x, p["norm_mix"][1], w["w_q"], tm=tm_mm, tn=tn_mm, head_g=p["b_q_norm"][0],
                    n_normed_cols=hd, out_scale=ATT_HEAD_DIM ** -0.5 * LOG2E,
                    out_dtype=BF16 if prompt else F32, name="q_proj")
    if prompt:
        kv, kv_bf = kv_out
        o = flash_attention(q, kv_bf, logf_out[1], n_heads=n_heads, tq=512, heads_per_step=2)
    else:
        kv = kv_out
        cache_kv, cache_logf, page_table = attend
        o = paged_attention(q.reshape(batch, seq, hd), kv.reshape(batch, seq, 2 * hd),
                            logf[:, :n_heads].reshape(batch, seq, n_heads), cache_kv, cache_logf,
                            page_table, pages_per_step=4).reshape(batch * seq, hd)
    x = matmul_res(o, w["w_o"], x, tm=tm_mm, tn=tn_mm, name="o_proj")
    x = _moe_layer(x, p, w, 1, tm)
    return x, ssm_new, conv_new, kv, logf[:, :n_heads]


def kernel(x_prompt, x_sample, state_ssm, state_conv, cache_kv, cache_logf, page_table, norm_mix, norm_ffn, a_w_in, a_conv_w, a_conv_b, a_dt_bias, a_a_log, a_d_skip, a_norm, a_w_out, kv_norm, w_kv, b_forget, k_norm, b_w_q, b_q_norm, b_w_o, m_w_group, m_b_group, m_w_expert, m_b_expert, m_w_gate, m_w_up, m_w_down):
    p = dict(norm_mix=norm_mix, norm_ffn=norm_ffn, a_w_in=a_w_in, a_conv_w=a_conv_w,
             a_conv_b=a_conv_b, a_dt_bias=a_dt_bias, a_a_log=a_a_log, a_d_skip=a_d_skip,
             a_norm=a_norm, a_w_out=a_w_out, kv_norm=kv_norm, w_kv=w_kv, b_forget=b_forget,
             k_norm=k_norm, b_w_q=b_w_q, b_q_norm=b_q_norm, b_w_o=b_w_o, m_w_group=m_w_group,
             m_b_group=m_b_group, m_w_expert=m_w_expert, m_b_expert=m_b_expert,
             m_w_gate=m_w_gate, m_w_up=m_w_up, m_w_down=m_w_down)
    assert a_w_in.shape[0] == 1 and b_w_q.shape[0] == 1, "one SSD layer then one attention layer"
    w = _prep_weights(p)
    d_model, d_inner, hd = w["d_model"], w["d_inner"], w["hd"]
    n_heads = hd // ATT_HEAD_DIM
    conv_dim = state_conv.shape[-1]
    ssm_heads = state_ssm.shape[2]

    b_p, seq_p, _ = x_prompt.shape
    assert b_p == 1, "the prompt path carries one cumulative forget sum over its rows"
    conv0 = jnp.zeros((b_p, CONV_WIDTH - 1, conv_dim), F32)
    ssm0 = jnp.zeros((b_p, ssm_heads, SSM_HEAD_DIM, SSM_STATE), F32)
    y_p, ssm_p, conv_p, kv_p, logf_p = _trunk(
        x_prompt.reshape(b_p * seq_p, d_model), conv0, ssm0, p, w,
        batch=b_p, seq=seq_p, tm=512, attend="prompt")

    b_s, seq_s, _ = x_sample.shape
    y_s, ssm_s, conv_s, kv_s, logf_s = _trunk(
        x_sample.reshape(b_s * seq_s, d_model), state_conv[0], state_ssm[0], p, w,
        batch=b_s, seq=seq_s, tm=b_s * seq_s, attend=(cache_kv, cache_logf, page_table))

    return (y_p.reshape(b_p, seq_p, d_model), y_s.reshape(b_s, seq_s, d_model),
            ssm_p[None], conv_p[None],
            kv_p.reshape(b_p, seq_p, 2, n_heads, ATT_HEAD_DIM), logf_p.reshape(b_p, seq_p, n_heads),
            ssm_s[None], conv_s[None],
            kv_s.reshape(b_s, seq_s, 2, n_heads, ATT_HEAD_DIM), logf_s.reshape(b_s, seq_s, n_heads))
```

```python
import functools

import jax
import jax.numpy as jnp
from jax import lax
from jax.experimental import pallas as pl
from jax.experimental.pallas import tpu as pltpu

F32 = jnp.float32
BF16 = jnp.bfloat16

LANES = 128
SUBLANES = 8
VMEM_LIMIT = 56 * 1024 * 1024

RMS_EPS = 1e-5
LOG2E = 1.4426950408889634
SSM_HEAD_DIM = 64
SSM_GROUPS = 8
SSM_STATE = 128
CONV_WIDTH = 4
ATT_HEAD_DIM = 128
N_EXPERT_GROUPS = 4
EXPERTS_PER_GROUP = 4
N_EXPERTS = N_EXPERT_GROUPS * EXPERTS_PER_GROUP


def _params(*sem):
    return pltpu.CompilerParams(dimension_semantics=sem, vmem_limit_bytes=VMEM_LIMIT)


def _dot(a, b):
    return jnp.dot(a, b, preferred_element_type=F32)


def _dot_nt(a, b):
    return lax.dot_general(a, b, (((1,), (1,)), ((), ())), preferred_element_type=F32)


def _dot_tn(a, b):
    return lax.dot_general(a, b, (((0,), (0,)), ((), ())), preferred_element_type=F32)


def _split_bf16(v, terms):
    out = []
    r = v
    for _ in range(terms):
        t = r.astype(BF16)
        out.append(t)
        r = r - t.astype(F32)
    return out


def _mm(a, b, dot, precise):
    if not precise:
        return dot(a.astype(BF16), b.astype(BF16))
    a_hi, a_lo = _split_bf16(a, 2)
    b_hi, b_lo = _split_bf16(b, 2)
    return dot(a_hi, b_hi) + (dot(a_lo, b_hi) + dot(a_hi, b_lo))


def _silu(x):
    return x / (1.0 + jnp.exp(-x))


def _softplus(x):
    return jnp.maximum(x, 0.0) + jnp.log1p(jnp.exp(-jnp.abs(x)))


def _rms_rows(x, g):
    ms = jnp.mean(x * x, axis=-1, keepdims=True)
    return x * lax.rsqrt(ms + RMS_EPS) * g


def _norm_matmul_kernel(*refs, n_normed_tiles, has_head_norm, out_scale, bf16_copy, precise):
    refs = list(refs)
    h_scr = refs.pop()
    o2_ref = refs.pop() if bf16_copy else None
    o_ref = refs.pop()
    hg_ref = refs.pop() if has_head_norm else None
    x_ref, g_ref, w_ref = refs
    j = pl.program_id(1)

    @pl.when(j == 0)
    def _():
        h_scr[...] = _rms_rows(x_ref[...], g_ref[...]).astype(h_scr.dtype)

    acc = _mm(h_scr[...], w_ref[...], _dot, precise)
    tn = acc.shape[1]

    def head_normed(a):
        parts = []
        for c in range(tn // ATT_HEAD_DIM):
            blk = a[:, c * ATT_HEAD_DIM:(c + 1) * ATT_HEAD_DIM]
            parts.append(_rms_rows(blk, hg_ref[...]))
        return jnp.concatenate(parts, axis=1)

    def emit(a):
        if out_scale != 1.0:
            a = a * out_scale
        o_ref[...] = a.astype(o_ref.dtype)
        if bf16_copy:
            o2_ref[...] = a.astype(BF16)

    if not has_head_norm:
        emit(acc)
    else:
        @pl.when(j < n_normed_tiles)
        def _():
            emit(head_normed(acc))

        @pl.when(j >= n_normed_tiles)
        def _():
            emit(acc)


def norm_matmul(x, g, w, *, tm, tn, head_g=None, n_normed_cols=0, out_scale=1.0, out_dtype=F32,
                bf16_copy=False, precise=False, name="norm_matmul"):
    m, k = x.shape
    n = w.shape[1]
    assert m % tm == 0 and n % tn == 0 and n_normed_cols % tn == 0
    has_hn = head_g is not None
    in_specs = [pl.BlockSpec((tm, k), lambda i, j: (i, 0)),
                pl.BlockSpec((1, k), lambda i, j: (0, 0)),
                pl.BlockSpec((k, tn), lambda i, j: (0, j))]
    args = [x, g.reshape(1, k), w]
    if has_hn:
        in_specs.append(pl.BlockSpec((1, ATT_HEAD_DIM), lambda i, j: (0, 0)))
        args.append(head_g.reshape(1, ATT_HEAD_DIM))
    out_specs = [pl.BlockSpec((tm, tn), lambda i, j: (i, j))] * 2
    out_shape = [jax.ShapeDtypeStruct((m, n), out_dtype), jax.ShapeDtypeStruct((m, n), BF16)]
    return pl.pallas_call(
        functools.partial(_norm_matmul_kernel, n_normed_tiles=n_normed_cols // tn,
                          has_head_norm=has_hn, out_scale=out_scale, bf16_copy=bf16_copy,
                          precise=precise),
        grid=(m // tm, n // tn),
        in_specs=in_specs,
        out_specs=out_specs if bf16_copy else out_specs[0],
        out_shape=out_shape if bf16_copy else out_shape[0],
        scratch_shapes=[pltpu.VMEM((tm, k), F32 if precise else BF16)],
        compiler_params=_params("parallel", "arbitrary"),
        name=name,
    )(*args)


def _matmul_res_kernel(a_ref, w_ref, r_ref, o_ref, *, precise):
    o_ref[...] = r_ref[...] + _mm(a_ref[...], w_ref[...], _dot, precise)


def matmul_res(a, w, res, *, tm, tn, precise=False, name="matmul_res"):
    m, k = a.shape
    n = w.shape[1]
    assert m % tm == 0 and n % tn == 0
    return pl.pallas_call(
        functools.partial(_matmul_res_kernel, precise=precise),
        grid=(m // tm, n // tn),
        in_specs=[pl.BlockSpec((tm, k), lambda i, j: (i, 0)),
                  pl.BlockSpec((k, tn), lambda i, j: (0, j)),
                  pl.BlockSpec((tm, tn), lambda i, j: (i, j))],
        out_specs=pl.BlockSpec((tm, tn), lambda i, j: (i, j)),
        out_shape=jax.ShapeDtypeStruct((m, n), F32),
        compiler_params=_params("parallel", "arbitrary"),
        name=name,
    )(a, w, res)


def _ssd_kernel(z_ref, xs_ref, b_ref, c_ref, dt_ref, conv0_ref, ssm0_ref, cw_ref, cbias_ref,
                dtb_ref, alog_ref, dskip_ref, gn_ref, expand_ref, tril_ref, eye_ref,
                yn_ref, ssm_out_ref, conv_out_ref,
                ext_scr, act_scr, y_scr, h_scr, *, chunk, n_heads, d_inner, gn_cols, precise):
    q = chunk
    c = pl.program_id(1)
    nc = pl.num_programs(1)
    pad = SUBLANES
    heads_per_group = n_heads // SSM_GROUPS
    gcols = heads_per_group * SSM_HEAD_DIM

    @pl.when(c == 0)
    def _():
        ext_scr[0:pad, :] = conv0_ref[0]
        h_scr[...] = ssm0_ref[0]

    ext_scr[pad:pad + q, 0:d_inner] = xs_ref[...]
    ext_scr[pad:pad + q, d_inner:d_inner + gn_cols] = b_ref[...]
    ext_scr[pad:pad + q, d_inner + gn_cols:] = c_ref[...]

    acc = jnp.broadcast_to(cbias_ref[...], (q, ext_scr.shape[1]))
    for tap in range(CONV_WIDTH):
        start = pad - (CONV_WIDTH - 1) + tap
        acc = acc + ext_scr[start:start + q, :] * cw_ref[tap:tap + 1, :]
    act_scr[...] = _silu(acc)

    tail = ext_scr[q:q + pad, :]
    conv_out_ref[0] = tail
    ext_scr[0:pad, :] = tail

    dt = _softplus(dt_ref[...] + dtb_ref[...])
    a = dt * (-jnp.exp(alog_ref[...]))
    tril = tril_ref[...]
    eye = eye_ref[...]
    a_cum = sum(_dot(tril, t) for t in _split_bf16(a, 3))
    a_cum_t = sum(_dot_nt(eye, t) for t in _split_bf16(a_cum, 3))
    dt_t = sum(_dot_nt(eye, t) for t in _split_bf16(dt, 3))
    a_last = a_cum[q - 1:q, :]
    w_state = dt * jnp.exp(a_last - a_cum)
    e_cum = jnp.exp(a_cum)
    expand = expand_ref[...]
    w_state_x = sum(_dot(t, expand) for t in _split_bf16(w_state, 2))
    e_cum_x = sum(_dot(t, expand) for t in _split_bf16(e_cum, 2))

    row_i = lax.broadcasted_iota(jnp.int32, (q, q), 0)
    col_i = lax.broadcasted_iota(jnp.int32, (q, q), 1)
    causal = col_i <= row_i
    lane = lax.broadcasted_iota(jnp.int32, (q, LANES), 1)
    lo_half = lane < SSM_HEAD_DIM

    operand = (lambda v: v) if precise else (lambda v: v.astype(BF16))
    for g in range(SSM_GROUPS):
        bg = operand(act_scr[:, d_inner + g * SSM_STATE:d_inner + (g + 1) * SSM_STATE])
        cg = operand(act_scr[:, d_inner + gn_cols + g * SSM_STATE:
                             d_inner + gn_cols + (g + 1) * SSM_STATE])
        cb = _mm(cg, bg, _dot_nt, precise)
        for j in range(gcols // LANES):
            c0 = g * gcols + j * LANES
            xs_blk = act_scr[:, c0:c0 + LANES]
            y_pair = None
            for half in range(2):
                h = (c0 // SSM_HEAD_DIM) + half
                seg = a_cum[:, h:h + 1] - a_cum_t[h:h + 1, :]
                m = cb * jnp.exp(jnp.where(causal, seg, -jnp.inf)) * dt_t[h:h + 1, :]
                keep = lo_half if half == 0 else jnp.logical_not(lo_half)
                part = _mm(m, jnp.where(keep, xs_blk, 0.0), _dot, precise)
                y_pair = part if y_pair is None else y_pair + part
            y_scr[:, c0:c0 + LANES] = y_pair
        gs = slice(g * gcols, (g + 1) * gcols)
        h_g = h_scr[:, gs]
        y_off = _mm(cg, h_g, _dot, precise) * e_cum_x[:, gs]
        y_scr[:, gs] = y_scr[:, gs] + y_off
        xw = act_scr[:, gs] * w_state_x[:, gs]
        h_scr[:, gs] = h_g * e_cum_x[q - 1:q, gs] + _mm(bg, xw, _dot_tn, precise)

    y = y_scr[...] + act_scr[:, 0:d_inner] * dskip_ref[...]
    y = y * _silu(z_ref[...])
    gw = d_inner // SSM_GROUPS
    parts = []
    for g in range(SSM_GROUPS):
        parts.append(_rms_rows(y[:, g * gw:(g + 1) * gw], gn_ref[:, g * gw:(g + 1) * gw]))
    yn_ref[...] = jnp.concatenate(parts, axis=1).astype(yn_ref.dtype)

    @pl.when(c == nc - 1)
    def _():
        ssm_out_ref[0] = h_scr[...]


def ssd_mixer(proj, conv_state, ssm_state, conv_w, conv_b, dt_bias, a_log, d_skip, gn,
              *, batch, seq, chunk, d_inner, precise=False, name="ssd"):
    n_heads = d_inner // SSM_HEAD_DIM
    gn_cols = SSM_GROUPS * SSM_STATE
    conv_dim = d_inner + 2 * gn_cols
    nc = seq // chunk
    assert seq % chunk == 0 and n_heads <= LANES
    q = chunk
    pad = SUBLANES

    conv0 = jnp.concatenate(
        [jnp.zeros((batch, pad - (CONV_WIDTH - 1), conv_dim), F32), conv_state], axis=1)
    ssm0 = jnp.transpose(ssm_state.reshape(batch, d_inner, SSM_STATE), (0, 2, 1))

    def lane_pad(v):
        return jnp.pad(v, (0, LANES - v.shape[0])).reshape(1, LANES)

    head_of_col = jnp.arange(d_inner) // SSM_HEAD_DIM
    expand = (jnp.arange(LANES)[:, None] == head_of_col[None, :]).astype(BF16)
    tril = jnp.tril(jnp.ones((q, q), BF16))
    eye = jnp.eye(LANES, dtype=BF16)
    dskip_cols = jnp.repeat(d_skip, SSM_HEAD_DIM).reshape(1, d_inner)

    zb = d_inner // d_inner
    row = lambda b, c: b * nc + c
    const2 = lambda b, c: (0, 0)
    out_dtype = BF16 if (q % 16 == 0 and not precise) else F32
    yn, ssm_t, conv_tail = pl.pallas_call(
        functools.partial(_ssd_kernel, chunk=q, n_heads=n_heads, d_inner=d_inner, gn_cols=gn_cols,
                          precise=precise),
        grid=(batch, nc),
        in_specs=[
            pl.BlockSpec((q, d_inner), lambda b, c: (row(b, c), 0)),
            pl.BlockSpec((q, d_inner), lambda b, c: (row(b, c), zb)),
            pl.BlockSpec((q, gn_cols), lambda b, c: (row(b, c), 2 * d_inner // gn_cols)),
            pl.BlockSpec((q, gn_cols), lambda b, c: (row(b, c), 2 * d_inner // gn_cols + 1)),
            pl.BlockSpec((q, LANES), lambda b, c: (row(b, c), (2 * d_inner + 2 * gn_cols) // LANES)),
            pl.BlockSpec((1, pad, conv_dim), lambda b, c: (b, 0, 0)),
            pl.BlockSpec((1, SSM_STATE, d_inner), lambda b, c: (b, 0, 0)),
            pl.BlockSpec((CONV_WIDTH, conv_dim), const2),
            pl.BlockSpec((1, conv_dim), const2),
            pl.BlockSpec((1, LANES), const2),
            pl.BlockSpec((1, LANES), const2),
            pl.BlockSpec((1, d_inner), const2),
            pl.BlockSpec((1, d_inner), const2),
            pl.BlockSpec((LANES, d_inner), const2),
            pl.BlockSpec((q, q), const2),
            pl.BlockSpec((LANES, LANES), const2),
        ],
        out_specs=[
            pl.BlockSpec((q, d_inner), lambda b, c: (row(b, c), 0)),
            pl.BlockSpec((1, SSM_STATE, d_inner), lambda b, c: (b, 0, 0)),
            pl.BlockSpec((1, pad, conv_dim), lambda b, c: (b, 0, 0)),
        ],
        out_shape=[
            jax.ShapeDtypeStruct((batch * seq, d_inner), out_dtype),
            jax.ShapeDtypeStruct((batch, SSM_STATE, d_inner), F32),
            jax.ShapeDtypeStruct((batch, pad, conv_dim), F32),
        ],
        scratch_shapes=[
            pltpu.VMEM((q + pad, conv_dim), F32),
            pltpu.VMEM((q, conv_dim), F32),
            pltpu.VMEM((q, d_inner), F32),
            pltpu.VMEM((SSM_STATE, d_inner), F32),
        ],
        compiler_params=_params("parallel", "arbitrary"),
        name=name,
    )(proj, proj, proj, proj, proj, conv0, ssm0, conv_w, conv_b.reshape(1, conv_dim),
      lane_pad(dt_bias), lane_pad(a_log), dskip_cols, gn.reshape(1, d_inner), expand, tril, eye)
    ssm_new = jnp.transpose(ssm_t, (0, 2, 1)).reshape(batch, n_heads, SSM_HEAD_DIM, SSM_STATE)
    return yn, ssm_new, conv_tail[:, pad - (CONV_WIDTH - 1):]


def _router_kernel(x_ref, g_ref, whi_ref, wlo_ref, bias_ref, gates_ref, hn_ref):
    h = _rms_rows(x_ref[...], g_ref[...])
    hn_ref[...] = h.astype(BF16)
    h_hi, h_lo = _split_bf16(h, 2)
    logits = (_dot(h_hi, whi_ref[...]) + _dot(h_lo, whi_ref[...]) + _dot(h_hi, wlo_ref[...])
              + bias_ref[...])
    tm = logits.shape[0]
    lane = lax.broadcasted_iota(jnp.int32, (tm, LANES), 1)
    neg = -jnp.inf
    is_group = lane < N_EXPERT_GROUPS
    gl = jnp.where(is_group, logits, neg)
    g_max = jnp.max(gl, axis=-1, keepdims=True)
    g_idx = jnp.min(jnp.where(gl == g_max, lane, LANES), axis=-1, keepdims=True)
    g_w = 1.0 / jnp.sum(jnp.where(is_group, jnp.exp(gl - g_max), 0.0), axis=-1, keepdims=True)
    first = N_EXPERT_GROUPS + g_idx * EXPERTS_PER_GROUP
    in_group = (lane >= first) & (lane < first + EXPERTS_PER_GROUP)
    el = jnp.where(in_group, logits, neg)
    v1 = jnp.max(el, axis=-1, keepdims=True)
    i1 = jnp.min(jnp.where(el == v1, lane, LANES), axis=-1, keepdims=True)
    el2 = jnp.where(lane == i1, neg, el)
    v2 = jnp.max(el2, axis=-1, keepdims=True)
    i2 = jnp.min(jnp.where(el2 == v2, lane, LANES), axis=-1, keepdims=True)
    e2 = jnp.exp(v2 - v1)
    w1 = g_w / (1.0 + e2)
    w2 = g_w * e2 / (1.0 + e2)
    gates = jnp.where(lane == i1, w1, 0.0) + jnp.where(lane == i2, w2, 0.0)
    gates = pltpu.roll(gates, LANES - N_EXPERT_GROUPS, 1)
    gates_ref[...] = jnp.where(lane == N_EXPERTS, g_idx.astype(F32), gates)


def moe_router(x, g, w_group, b_group, w_expert, b_expert, *, tm, name="router"):
    m, k = x.shape
    w = jnp.concatenate([w_group, w_expert.reshape(k, N_EXPERTS)], axis=1)
    w = jnp.pad(w, ((0, 0), (0, LANES - w.shape[1])))
    w_hi = w.astype(BF16)
    w_lo = (w - w_hi.astype(F32)).astype(BF16)
    bias = jnp.pad(jnp.concatenate([b_group, b_expert.reshape(N_EXPERTS)]),
                   (0, LANES - N_EXPERT_GROUPS - N_EXPERTS)).reshape(1, LANES)
    return pl.pallas_call(
        _router_kernel,
        grid=(m // tm,),
        in_specs=[pl.BlockSpec((tm, k), lambda i: (i, 0)),
                  pl.BlockSpec((1, k), lambda i: (0, 0)),
                  pl.BlockSpec((k, LANES), lambda i: (0, 0)),
                  pl.BlockSpec((k, LANES), lambda i: (0, 0)),
                  pl.BlockSpec((1, LANES), lambda i: (0, 0))],
        out_specs=[pl.BlockSpec((tm, LANES), lambda i: (i, 0)),
                   pl.BlockSpec((tm, k), lambda i: (i, 0))],
        out_shape=[jax.ShapeDtypeStruct((m, LANES), F32),
                   jax.ShapeDtypeStruct((m, k), BF16)],
        compiler_params=_params("parallel"),
        name=name,
    )(x, g.reshape(1, k), w_hi, w_lo, bias)


MOE_SUB_ROWS = 128


def _moe_kernel(hn_ref, gates_ref, x_ref, wg_ref, wu_ref, wd_ref, lower_ref, upper_ref, eye_ref,
                o_ref, hs_scr, gs_scr, acc_scr, perm_t_scr, bounds_scr):
    e = pl.program_id(1)
    n_e = pl.num_programs(1)
    tm = hn_ref.shape[0]
    lane = lax.broadcasted_iota(jnp.int32, (tm, LANES), 1)

    @pl.when(e == 0)
    def _():
        gates = gates_ref[...]
        grp = gates[:, N_EXPERTS:N_EXPERTS + 1]
        onehot = jnp.where((lane.astype(F32) == grp) & (lane < N_EXPERT_GROUPS), 1.0, 0.0)
        counts = jnp.sum(onehot, axis=0, keepdims=True)
        starts = sum(_dot(t, upper_ref[...]) for t in _split_bf16(counts, 3))
        rank = _dot(lower_ref[...], onehot.astype(BF16))
        pos = jnp.sum(onehot * (starts + rank), axis=-1, keepdims=True)
        pos_rows = sum(_dot_nt(eye_ref[...], t)
                       for t in _split_bf16(jnp.broadcast_to(pos, (tm, LANES)), 3))
        r_i = lax.broadcasted_iota(jnp.int32, (tm, tm), 0).astype(F32)
        c_i = lax.broadcasted_iota(jnp.int32, (tm, tm), 1).astype(F32)
        perm = jnp.where(r_i == pos_rows[0:1, :], 1.0, 0.0).astype(BF16)
        perm_t_scr[...] = jnp.where(c_i == pos, 1.0, 0.0).astype(BF16)
        hs_scr[...] = _dot(perm, hn_ref[...]).astype(BF16)
        gs_scr[...] = sum(_dot(perm, t) for t in _split_bf16(gates, 3))
        acc_scr[...] = jnp.zeros_like(acc_scr)
        ends = starts + counts
        for g in range(N_EXPERT_GROUPS):
            bounds_scr[g] = jnp.sum(jnp.where(lane[0:1] == g, starts, 0.0)).astype(jnp.int32)
            bounds_scr[N_EXPERT_GROUPS + g] = jnp.sum(
                jnp.where(lane[0:1] == g, ends, 0.0)).astype(jnp.int32)

    g_e = e // EXPERTS_PER_GROUP
    row_lo = bounds_scr[g_e]
    row_hi = bounds_scr[N_EXPERT_GROUPS + g_e]
    sub = min(MOE_SUB_ROWS, tm)
    sub_lane = lax.broadcasted_iota(jnp.int32, (sub, LANES), 1)
    for s in range(tm // sub):
        r0 = s * sub

        @pl.when((r0 < row_hi) & (r0 + sub > row_lo))
        def _():
            h = hs_scr[r0:r0 + sub, :]
            gate = jnp.sum(jnp.where(sub_lane == e, gs_scr[r0:r0 + sub, :], 0.0),
                           axis=-1, keepdims=True)
            hidden = _silu(_dot(h, wg_ref[0])) * _dot(h, wu_ref[0]) * gate
            acc_scr[r0:r0 + sub, :] += _dot(hidden.astype(BF16), wd_ref[0])

    @pl.when(e == n_e - 1)
    def _():
        y = sum(_dot(perm_t_scr[...], t) for t in _split_bf16(acc_scr[...], 2))
        o_ref[...] = x_ref[...] + y


def moe_ffn(hn, gates, x, w_gate, w_up, w_down, *, tm, name="moe"):
    m, k = hn.shape
    n_e, _, f = w_gate.shape
    assert tm % min(MOE_SUB_ROWS, tm) == 0 and n_e == N_EXPERTS
    lower = jnp.tril(jnp.ones((tm, tm), BF16), k=-1)
    upper = jnp.triu(jnp.ones((LANES, LANES), BF16), k=1)
    eye = jnp.eye(LANES, dtype=BF16)
    const = lambda i, e: (0, 0)
    return pl.pallas_call(
        _moe_kernel,
        grid=(m // tm, n_e),
        in_specs=[pl.BlockSpec((tm, k), lambda i, e: (i, 0)),
                  pl.BlockSpec((tm, LANES), lambda i, e: (i, 0)),
                  pl.BlockSpec((tm, k), lambda i, e: (i, 0)),
                  pl.BlockSpec((1, k, f), lambda i, e: (e, 0, 0)),
                  pl.BlockSpec((1, k, f), lambda i, e: (e, 0, 0)),
                  pl.BlockSpec((1, f, k), lambda i, e: (e, 0, 0)),
                  pl.BlockSpec((tm, tm), const),
                  pl.BlockSpec((LANES, LANES), const),
                  pl.BlockSpec((LANES, LANES), const)],
        out_specs=pl.BlockSpec((tm, k), lambda i, e: (i, 0)),
        out_shape=jax.ShapeDtypeStruct((m, k), F32),
        scratch_shapes=[pltpu.VMEM((tm, k), BF16), pltpu.VMEM((tm, LANES), F32),
                        pltpu.VMEM((tm, k), F32), pltpu.VMEM((tm, tm), BF16),
                        pltpu.SMEM((2 * N_EXPERT_GROUPS,), jnp.int32)],
        compiler_params=_params("parallel", "arbitrary"),
        name=name,
    )(hn, gates, x, w_gate, w_up, w_down, lower, upper, eye)


def _logf_kernel(x_ref, g_ref, w_ref, b_ref, *rest, n_heads, with_bias_cols):
    h = _rms_rows(x_ref[...], g_ref[...]).astype(BF16)
    zf = _dot(h, w_ref[...]) + b_ref[...]
    logf = -_softplus(-zf)
    if not with_bias_cols:
        (logf_ref,) = rest
        logf_ref[...] = logf
        return
    tril_ref, logf_ref, bias_ref, carry_scr = rest
    logf_ref[...] = logf

    @pl.when(pl.program_id(0) == 0)
    def _():
        carry_scr[...] = jnp.zeros_like(carry_scr)

    cum = sum(_dot(tril_ref[...], t) for t in _split_bf16(logf, 3)) + carry_scr[...]
    tm = cum.shape[0]
    carry_scr[...] = cum[tm - 1:tm, :]
    lane = lax.broadcasted_iota(jnp.int32, (tm, LANES), 1)
    for hd in range(n_heads):
        t0, t1, t2 = [t.astype(F32) for t in _split_bf16(cum[:, hd:hd + 1] * (-LOG2E), 3)]
        blk = jnp.where(lane == 0, t0, jnp.where(lane == 1, t1, jnp.where(lane == 2, t2, 0.0)))
        bias_ref[:, hd * LANES:(hd + 1) * LANES] = blk.astype(BF16)


def logf_proj(x, g, w_f, b_f, *, tm, with_bias_cols, name="logf"):
    m, k = x.shape
    n_h = w_f.shape[1]
    w = jnp.pad(w_f, ((0, 0), (0, LANES - n_h))).astype(BF16)
    b = jnp.pad(b_f, (0, LANES - n_h)).reshape(1, LANES)
    in_specs = [pl.BlockSpec((tm, k), lambda i: (i, 0)),
                pl.BlockSpec((1, k), lambda i: (0, 0)),
                pl.BlockSpec((k, LANES), lambda i: (0, 0)),
                pl.BlockSpec((1, LANES), lambda i: (0, 0))]
    args = [x, g.reshape(1, k), w, b]
    out_specs = [pl.BlockSpec((tm, LANES), lambda i: (i, 0))]
    out_shape = [jax.ShapeDtypeStruct((m, LANES), F32)]
    scratch = []
    if with_bias_cols:
        in_specs.append(pl.BlockSpec((tm, tm), lambda i: (0, 0)))
        args.append(jnp.tril(jnp.ones((tm, tm), BF16)))
        out_specs.append(pl.BlockSpec((tm, n_h * LANES), lambda i: (i, 0)))
        out_shape.append(jax.ShapeDtypeStruct((m, n_h * LANES), BF16))
        scratch.append(pltpu.VMEM((1, LANES), F32))
    return pl.pallas_call(
        functools.partial(_logf_kernel, n_heads=n_h, with_bias_cols=with_bias_cols),
        grid=(m // tm,),
        in_specs=in_specs,
        out_specs=out_specs,
        out_shape=out_shape,
        scratch_shapes=scratch,
        compiler_params=_params("arbitrary"),
        name=name,
    )(*args)


def _flash_kernel(q_ref, k_ref, v_ref, bias_ref, o_ref, vt_scr, m_scr, l_scr, acc_scr, *,
                  tq, heads_per_step):
    qi = pl.program_id(1)
    d = ATT_HEAD_DIM
    n_blocks = k_ref.shape[0] // tq

    @pl.when(qi == 0)
    def _():
        for hh in range(heads_per_step):
            for c in range(n_blocks):
                blk = v_ref[c * tq:(c + 1) * tq, hh * d:(hh + 1) * d].astype(F32)
                vt_scr[hh, c] = blk.T.astype(BF16)

    lane = lax.broadcasted_iota(jnp.int32, (tq, d), 1)
    ones_cols = jnp.where(lane < 3, 1.0, 0.0).astype(BF16)
    q_aug = [jnp.concatenate([q_ref[:, hh * d:(hh + 1) * d], ones_cols], axis=1)
             for hh in range(heads_per_step)]

    def block(ki, n_sub, masked, first):
        for hh in range(heads_per_step):
            scores = []
            for u in range(n_sub):
                start = pl.multiple_of((ki + u) * tq, tq)
                k_aug = jnp.concatenate([k_ref[pl.ds(start, tq), hh * d:(hh + 1) * d],
                                         bias_ref[pl.ds(start, tq), hh * d:(hh + 1) * d]], axis=1)
                s = _dot_nt(k_aug, q_aug[hh])
                if masked:
                    key_i = lax.broadcasted_iota(jnp.int32, (tq, tq), 0)
                    qry_i = lax.broadcasted_iota(jnp.int32, (tq, tq), 1)
                    s = jnp.where(key_i <= qry_i, s, -jnp.inf)
                scores.append(s)
            s_max = functools.reduce(jnp.maximum,
                                     [jnp.max(s, axis=0, keepdims=True) for s in scores])
            if first:
                m_new = s_max
            else:
                m_prev = m_scr[hh]
                m_new = jnp.maximum(m_prev, s_max)
                alpha = jnp.exp2(m_prev - m_new)
            probs = [jnp.exp2(s - m_new) for s in scores]
            l_new = sum(jnp.sum(p, axis=0, keepdims=True) for p in probs)
            pv = sum(_dot(vt_scr[hh, ki + u], p.astype(BF16)) for u, p in enumerate(probs))
            if first:
                l_scr[hh] = l_new
                acc_scr[hh] = pv
            else:
                l_scr[hh] = l_scr[hh] * alpha + l_new
                acc_scr[hh] = acc_scr[hh] * alpha + pv
            m_scr[hh] = m_new

    def pair_body(kp, carry):
        block(2 * kp, 2, False, False)
        return carry

    block(qi, 1, True, True)
    lax.fori_loop(0, qi // 2, pair_body, 0)

    @pl.when(qi % 2 == 1)
    def _():
        block(qi - 1, 1, False, False)
    for hh in range(heads_per_step):
        out_t = acc_scr[hh] / l_scr[hh]
        o_ref[:, hh * d:(hh + 1) * d] = out_t.T.astype(o_ref.dtype)


def flash_attention(q, kv, bias_cols, *, n_heads, tq, heads_per_step, name="flash"):
    seq = q.shape[0]
    d = ATT_HEAD_DIM
    w = heads_per_step * d
    n_steps = n_heads // heads_per_step
    assert n_heads % heads_per_step == 0 and seq % tq == 0
    return pl.pallas_call(
        functools.partial(_flash_kernel, tq=tq, heads_per_step=heads_per_step),
        grid=(n_steps, seq // tq),
        in_specs=[pl.BlockSpec((tq, w), lambda h, i: (i, h)),
                  pl.BlockSpec((seq, w), lambda h, i: (0, h)),
                  pl.BlockSpec((seq, w), lambda h, i: (0, n_steps + h)),
                  pl.BlockSpec((seq, w), lambda h, i: (0, h))],
        out_specs=pl.BlockSpec((tq, w), lambda h, i: (i, h)),
        out_shape=jax.ShapeDtypeStruct((seq, n_heads * d), BF16),
        scratch_shapes=[pltpu.VMEM((heads_per_step, seq // tq, d, tq), BF16),
                        pltpu.VMEM((heads_per_step, 1, tq), F32),
                        pltpu.VMEM((heads_per_step, 1, tq), F32),
                        pltpu.VMEM((heads_per_step, d, tq), F32)],
        compiler_params=_params("parallel", "arbitrary"),
        name=name,
    )(q, kv, kv, bias_cols)


def _page_sums_kernel(lf_ref, upper_ref, ones_ref, inner_ref, total_ref):
    terms = _split_bf16(lf_ref[...], 3)
    inner_ref[...] = sum(_dot(upper_ref[...], t) for t in terms) * LOG2E
    total_ref[...] = sum(_dot(ones_ref[...], t) for t in terms) * LOG2E


def page_forget_sums(cache_logf, name="page_sums"):
    n_pool, page, n_heads = cache_logf.shape
    cols = n_pool * n_heads
    lf_t = jnp.transpose(cache_logf, (1, 0, 2)).reshape(page, cols)
    tn = max([t for t in range(LANES, 4096 + 1, LANES) if cols % t == 0], default=cols)
    return pl.pallas_call(
        _page_sums_kernel,
        grid=(cols // tn,),
        in_specs=[pl.BlockSpec((page, tn), lambda i: (0, i)),
                  pl.BlockSpec((page, page), lambda i: (0, 0)),
                  pl.BlockSpec((SUBLANES, page), lambda i: (0, 0))],
        out_specs=[pl.BlockSpec((page, tn), lambda i: (0, i)),
                   pl.BlockSpec((SUBLANES, tn), lambda i: (0, i))],
        out_shape=[jax.ShapeDtypeStruct((page, cols), F32),
                   jax.ShapeDtypeStruct((SUBLANES, cols), F32)],
        compiler_params=_params("parallel"),
        name=name,
    )(lf_t, jnp.triu(jnp.ones((page, page), BF16), k=1), jnp.ones((SUBLANES, page), BF16))


def _new_token_bias_kernel(lf_ref, tril_ref, o_ref):
    o_ref[...] = sum(_dot(tril_ref[...], t) for t in _split_bf16(lf_ref[...], 3)) * (-LOG2E)


def new_token_bias(logf_new, name="new_token_bias"):
    n_b, t_pad, n_heads = logf_new.shape
    rows = n_b * t_pad
    r = jnp.arange(rows)
    same_request = (r[:, None] // t_pad) == (r[None, :] // t_pad)
    tril = (same_request & (r[None, :] <= r[:, None])).astype(BF16)
    out = pl.pallas_call(
        _new_token_bias_kernel,
        out_shape=jax.ShapeDtypeStruct((rows, n_heads), F32),
        compiler_params=pltpu.CompilerParams(vmem_limit_bytes=VMEM_LIMIT),
        name=name,
    )(logf_new.reshape(rows, n_heads), tril)
    return out.reshape(n_b, t_pad, n_heads)


def _paged_attn_kernel(pt_ref, q_ref, kvn_ref, bn_ref, mask_new_ref, mask_page_ref, *rest,
                       n_heads, pages_per_step):
    kv_refs = rest[:pages_per_step]
    inner_refs = rest[pages_per_step:2 * pages_per_step]
    total_refs = rest[2 * pages_per_step:3 * pages_per_step]
    o_ref, m_scr, l_scr, acc_scr, carry_scr = rest[3 * pages_per_step:]
    del pt_ref
    j = pl.program_id(1)
    n_steps = pl.num_programs(1)
    n_half = n_heads // SUBLANES

    def rows_of(kv_view, first_head):
        n_keys = kv_view.shape[0]
        blk = kv_view[:, first_head:first_head + SUBLANES, :]
        return blk.reshape(n_keys * SUBLANES, ATT_HEAD_DIM).astype(BF16)

    def attend(kv_views, half, bias_rows, mask, first):
        lo = half * SUBLANES
        scores = [_dot_nt(q_ref[0, half], rows_of(view, lo)) + mask + bias
                  for view, bias in zip(kv_views, bias_rows)]
        s_max = functools.reduce(jnp.maximum, [jnp.max(s, axis=-1, keepdims=True) for s in scores])
        if first:
            m_new = s_max
        else:
            m_prev = m_scr[half]
            m_new = jnp.maximum(m_prev, s_max)
            alpha = jnp.exp2(m_prev - m_new)
        probs = [jnp.exp2(s - m_new) for s in scores]
        l_new = sum(jnp.sum(p, axis=-1, keepdims=True) for p in probs)
        pv = sum(_dot(p.astype(BF16), rows_of(view, n_heads + lo))
                 for p, view in zip(probs, kv_views))
        if first:
            l_scr[half] = l_new
            acc_scr[half] = pv
        else:
            l_scr[half] = l_scr[half] * alpha + l_new
            acc_scr[half] = acc_scr[half] * alpha + pv
        m_scr[half] = m_new

    @pl.when(j == 0)
    def _():
        for half in range(n_half):
            attend([kvn_ref.at[0]], half, [bn_ref[0, half]], mask_new_ref[...], True)
        carry_scr[...] = jnp.zeros_like(carry_scr)

    for half in range(n_half):
        carry = carry_scr[half]
        biases = []
        for inner, total in zip(inner_refs, total_refs):
            biases.append(inner[0, half] + carry)
            carry = carry + total[0, half]
        carry_scr[half] = carry
        attend([r.at[0] for r in kv_refs], half, biases, mask_page_ref[...], False)

    @pl.when(j == n_steps - 1)
    def _():
        for half in range(n_half):
            o_ref[0, half] = acc_scr[half] / l_scr[half]


def paged_attention(q, kv_new, logf_new, cache_kv, cache_logf, page_table, *, pages_per_step,
                    name="paged_attn"):
    n_b, n_tok, hd = q.shape
    d = ATT_HEAD_DIM
    n_heads = hd // d
    n_half = n_heads // SUBLANES
    n_pool, page = cache_logf.shape[0], cache_logf.shape[1]
    n_pages = page_table.shape[1]
    rows = SUBLANES * n_tok
    t_pad = -(-n_tok // 16) * 16
    assert n_pages % pages_per_step == 0 and n_heads % SUBLANES == 0 and rows % 16 == 0
    n_steps = n_pages // pages_per_step

    inner, total = page_forget_sums(cache_logf)
    inner_rows = inner.reshape(page, n_pool, n_half, SUBLANES).transpose(1, 2, 0, 3)
    inner_rows = inner_rows.reshape(n_pool, n_half, 1, page * SUBLANES)
    total_rows = jnp.broadcast_to(total[0].reshape(n_pool, n_half, 1, SUBLANES),
                                  (n_pool, n_half, page, SUBLANES))
    total_rows = total_rows.reshape(n_pool, n_half, 1, page * SUBLANES)
    neg_c_new = new_token_bias(jnp.pad(logf_new, ((0, 0), (0, t_pad - n_tok), (0, 0))))
    bias_new = neg_c_new.reshape(n_b, t_pad, n_half, SUBLANES).transpose(0, 2, 1, 3)
    bias_new = bias_new.reshape(n_b, n_half, 1, t_pad * SUBLANES)

    row_head = jnp.arange(rows)[:, None] // n_tok
    row_tok = jnp.arange(rows)[:, None] % n_tok

    def mask_for(n_keys, causal):
        col = jnp.arange(n_keys * SUBLANES)[None, :]
        ok = (col % SUBLANES) == row_head
        if causal:
            ok = ok & ((col // SUBLANES) <= row_tok)
        return jnp.where(ok, 0.0, -jnp.inf).astype(F32)

    q_rows = q.reshape(n_b, n_tok, n_half, SUBLANES, d).transpose(0, 2, 3, 1, 4)
    q_rows = q_rows.reshape(n_b, n_half, rows, d).astype(BF16)
    cache4 = cache_kv.reshape(n_pool, page, 2 * n_heads, d)
    kvn = jnp.pad(kv_new.reshape(n_b, n_tok, 2 * n_heads, d),
                  ((0, 0), (0, t_pad - n_tok), (0, 0), (0, 0)))

    def page_pos(j, i):
        return n_pages - 1 - (j * pages_per_step + i)

    in_specs = [pl.BlockSpec((1, n_half, rows, d), lambda b, j, pt: (b, 0, 0, 0)),
                pl.BlockSpec((1, t_pad, 2 * n_heads, d), lambda b, j, pt: (b, 0, 0, 0)),
                pl.BlockSpec((1, n_half, 1, t_pad * SUBLANES), lambda b, j, pt: (b, 0, 0, 0)),
                pl.BlockSpec((rows, t_pad * SUBLANES), lambda b, j, pt: (0, 0)),
                pl.BlockSpec((rows, page * SUBLANES), lambda b, j, pt: (0, 0))]
    in_specs += [pl.BlockSpec((1, page, 2 * n_heads, d),
                              functools.partial(lambda b, j, pt, i: (pt[b, page_pos(j, i)], 0, 0, 0), i=i))
                 for i in range(pages_per_step)]
    sums_specs = [pl.BlockSpec((1, n_half, 1, page * SUBLANES),
                               functools.partial(lambda b, j, pt, i: (pt[b, page_pos(j, i)], 0, 0, 0), i=i))
                  for i in range(pages_per_step)]
    in_specs += sums_specs + sums_specs
    grid_spec = pltpu.PrefetchScalarGridSpec(
        num_scalar_prefetch=1,
        grid=(n_b, n_steps),
        in_specs=in_specs,
        out_specs=pl.BlockSpec((1, n_half, rows, d), lambda b, j, pt: (b, 0, 0, 0)),
        scratch_shapes=[pltpu.VMEM((n_half, rows, 1), F32), pltpu.VMEM((n_half, rows, 1), F32),
                        pltpu.VMEM((n_half, rows, d), F32),
                        pltpu.VMEM((n_half, 1, page * SUBLANES), F32)])
    o = pl.pallas_call(
        functools.partial(_paged_attn_kernel, n_heads=n_heads, pages_per_step=pages_per_step),
        grid_spec=grid_spec,
        out_shape=jax.ShapeDtypeStruct((n_b, n_half, rows, d), F32),
        compiler_params=_params("parallel", "arbitrary"),
        name=name,
    )(page_table, q_rows, kvn, bias_new, mask_for(t_pad, True), mask_for(page, False),
      *([cache4] * pages_per_step), *([inner_rows] * pages_per_step),
      *([total_rows] * pages_per_step))
    o = o.reshape(n_b, n_half, SUBLANES, n_tok, d).transpose(0, 3, 1, 2, 4)
    return o.reshape(n_b, n_tok, hd)


def _prep_weights(p):
    d_model = p["a_w_in"].shape[1]
    w_in = p["a_w_in"][0]
    n_in = w_in.shape[1]
    d_inner = p["a_w_out"].shape[1]
    n_dt = n_in - (2 * d_inner + 2 * SSM_GROUPS * SSM_STATE)
    w_in_f32 = jnp.pad(w_in, ((0, 0), (0, LANES - n_dt)))
    w_in = w_in_f32.astype(BF16)
    hd = p["b_w_q"].shape[2]
    return dict(
        w_in=w_in, w_in_f32=w_in_f32, w_out=p["a_w_out"][0].astype(BF16),
        w_kv=p["w_kv"][:, :2 * hd].astype(BF16), w_f=p["w_kv"][:, 2 * hd:],
        w_q=p["b_w_q"][0].astype(BF16), w_o=p["b_w_o"][0].astype(BF16),
        m_gate=p["m_w_gate"].astype(BF16), m_up=p["m_w_up"].astype(BF16),
        m_down=p["m_w_down"].astype(BF16), d_model=d_model, d_inner=d_inner, hd=hd)


def _moe_layer(x, p, w, layer, tm):
    gates, hn = moe_router(x, p["norm_ffn"][layer], p["m_w_group"][layer], p["m_b_group"][layer],
                           p["m_w_expert"][layer], p["m_b_expert"][layer], tm=tm,
                           name=f"router{layer}")
    return moe_ffn(hn, gates, x, w["m_gate"][layer], w["m_up"][layer], w["m_down"][layer], tm=tm,
                   name=f"moe{layer}")


def _trunk(x, conv_state, ssm_state, p, w, *, batch, seq, tm, attend):
    d_inner, hd = w["d_inner"], w["hd"]
    n_heads = hd // ATT_HEAD_DIM
    prompt = isinstance(attend, str)
    precise = not prompt
    n_in = w["w_in"].shape[1]
    if precise:
        tn_in = n_in // 27 if n_in % (27 * LANES) == 0 else LANES
    else:
        tn_in = n_in // 9 if n_in % (9 * LANES) == 0 else LANES
    m_rows = batch * seq
    tm_mm = 2 * tm if (prompt and m_rows % (2 * tm) == 0) else tm
    tn_mm = 1024 if prompt else 512
    proj = norm_matmul(x, p["norm_mix"][0], w["w_in_f32"] if precise else w["w_in"],
                       tm=tm_mm, tn=tn_in, precise=precise, name="in_proj")
    yn, ssm_new, conv_new = ssd_mixer(
        proj, conv_state, ssm_state, p["a_conv_w"][0], p["a_conv_b"][0], p["a_dt_bias"][0],
        p["a_a_log"][0], p["a_d_skip"][0], p["a_norm"][0],
        batch=batch, seq=seq, chunk=min(128, seq), d_inner=d_inner, precise=precise)
    x = matmul_res(yn, p["a_w_out"][0] if precise else w["w_out"], x, tm=tm_mm,
                   tn=256 if precise else 512, precise=precise, name="out_proj")
    x = _moe_layer(x, p, w, 0, tm)

    kv_out = norm_matmul(x, p["kv_norm"], w["w_kv"], tm=tm_mm, tn=tn_mm, head_g=p["k_norm"],
                         n_normed_cols=hd, bf16_copy=prompt, name="kv_proj")
    logf_out = logf_proj(x, p["kv_norm"], w["w_f"], p["b_forget"], tm=tm, with_bias_cols=prompt)
    logf = logf_out[0]
    q = norm_matmul(x, p["norm_mix"][1], w["w_q"], tm=tm_mm, tn=tn_mm, head_g=p["b_q_norm"][0],
                    n_normed_cols=hd, out_scale=ATT_HEAD_DIM ** -0.5 * LOG2E,
                    out_dtype=BF16 if prompt else F32, name="q_proj")
    if prompt:
        kv, kv_bf = kv_out
        o = flash_attention(q, kv_bf, logf_out[1], n_heads=n_heads, tq=512, heads_per_step=2)
    else:
        kv = kv_out
        cache_kv, cache_logf, page_table = attend
        o = paged_attention(q.reshape(batch, seq, hd), kv.reshape(batch, seq, 2 * hd),
                            logf[:, :n_heads].reshape(batch, seq, n_heads), cache_kv, cache_logf,
                            page_table, pages_per_step=4).reshape(batch * seq, hd)
    x = matmul_res(o, w["w_o"], x, tm=tm_mm, tn=tn_mm, name="o_proj")
    x = _moe_layer(x, p, w, 1, tm)
    return x, ssm_new, conv_new, kv, logf[:, :n_heads]


def kernel(x_prompt, x_sample, state_ssm, state_conv, cache_kv, cache_logf, page_table, norm_mix, norm_ffn, a_w_in, a_conv_w, a_conv_b, a_dt_bias, a_a_log, a_d_skip, a_norm, a_w_out, kv_norm, w_kv, b_forget, k_norm, b_w_q, b_q_norm, b_w_o, m_w_group, m_b_group, m_w_expert, m_b_expert, m_w_gate, m_w_up, m_w_down):
    p = dict(norm_mix=norm_mix, norm_ffn=norm_ffn, a_w_in=a_w_in, a_conv_w=a_conv_w,
             a_conv_b=a_conv_b, a_dt_bias=a_dt_bias, a_a_log=a_a_log, a_d_skip=a_d_skip,
             a_norm=a_norm, a_w_out=a_w_out, kv_norm=kv_norm, w_kv=w_kv, b_forget=b_forget,
             k_norm=k_norm, b_w_q=b_w_q, b_q_norm=b_q_norm, b_w_o=b_w_o, m_w_group=m_w_group,
             m_b_group=m_b_group, m_w_expert=m_w_expert, m_b_expert=m_b_expert,
             m_w_gate=m_w_gate, m_w_up=m_w_up, m_w_down=m_w_down)
    assert a_w_in.shape[0] == 1 and b_w_q.shape[0] == 1, "one SSD layer then one attention layer"
    w = _prep_weights(p)
    d_model, d_inner, hd = w["d_model"], w["d_inner"], w["hd"]
    n_heads = hd // ATT_HEAD_DIM
    conv_dim = state_conv.shape[-1]
    ssm_heads = state_ssm.shape[2]

    b_p, seq_p, _ = x_prompt.shape
    assert b_p == 1, "the prompt path carries one cumulative forget sum over its rows"
    conv0 = jnp.zeros((b_p, CONV_WIDTH - 1, conv_dim), F32)
    ssm0 = jnp.zeros((b_p, ssm_heads, SSM_HEAD_DIM, SSM_STATE), F32)
    y_p, ssm_p, conv_p, kv_p, logf_p = _trunk(
        x_prompt.reshape(b_p * seq_p, d_model), conv0, ssm0, p, w,
        batch=b_p, seq=seq_p, tm=512, attend="prompt")

    b_s, seq_s, _ = x_sample.shape
    y_s, ssm_s, conv_s, kv_s, logf_s = _trunk(
        x_sample.reshape(b_s * seq_s, d_model), state_conv[0], state_ssm[0], p, w,
        batch=b_s, seq=seq_s, tm=b_s * seq_s, attend=(cache_kv, cache_logf, page_table))

    return (y_p.reshape(b_p, seq_p, d_model), y_s.reshape(b_s, seq_s, d_model),
            ssm_p[None], conv_p[None],
            kv_p.reshape(b_p, seq_p, 2, n_heads, ATT_HEAD_DIM), logf_p.reshape(b_p, seq_p, n_heads),
            ssm_s[None], conv_s[None],
            kv_s.reshape(b_s, seq_s, 2, n_heads, ATT_HEAD_DIM), logf_s.reshape(b_s, seq_s, n_heads))
```

```python
import functools

import jax
import jax.numpy as jnp
from jax import lax
from jax.experimental import pallas as pl
from jax.experimental.pallas import tpu as pltpu

F32 = jnp.float32
BF16 = jnp.bfloat16

LANES = 128
SUBLANES = 8
VMEM_LIMIT = 56 * 1024 * 1024

RMS_EPS = 1e-5
LOG2E = 1.4426950408889634
SSM_HEAD_DIM = 64
SSM_GROUPS = 8
SSM_STATE = 128
CONV_WIDTH = 4
ATT_HEAD_DIM = 128
N_EXPERT_GROUPS = 4
EXPERTS_PER_GROUP = 4
N_EXPERTS = N_EXPERT_GROUPS * EXPERTS_PER_GROUP


def _params(*sem):
    return pltpu.CompilerParams(dimension_semantics=sem, vmem_limit_bytes=VMEM_LIMIT)


def _dot(a, b):
    return jnp.dot(a, b, preferred_element_type=F32)


def _dot_nt(a, b):
    return lax.dot_general(a, b, (((1,), (1,)), ((), ())), preferred_element_type=F32)


def _dot_tn(a, b):
    return lax.dot_general(a, b, (((0,), (0,)), ((), ())), preferred_element_type=F32)


def _split_bf16(v, terms):
    out = []
    r = v
    for _ in range(terms):
        t = r.astype(BF16)
        out.append(t)
        r = r - t.astype(F32)
    return out


def _mm(a, b, dot, precise):
    if not precise:
        return dot(a.astype(BF16), b.astype(BF16))
    a_hi, a_lo = _split_bf16(a, 2)
    b_hi, b_lo = _split_bf16(b, 2)
    return dot(a_hi, b_hi) + (dot(a_lo, b_hi) + dot(a_hi, b_lo))


def _silu(x):
    return x / (1.0 + jnp.exp(-x))


def _softplus(x):
    return jnp.maximum(x, 0.0) + jnp.log1p(jnp.exp(-jnp.abs(x)))


def _rms_rows(x, g):
    ms = jnp.mean(x * x, axis=-1, keepdims=True)
    return x * lax.rsqrt(ms + RMS_EPS) * g


def _norm_matmul_kernel(*refs, n_normed_tiles, has_head_norm, out_scale, bf16_copy, precise):
    refs = list(refs)
    h_scr = refs.pop()
    o2_ref = refs.pop() if bf16_copy else None
    o_ref = refs.pop()
    hg_ref = refs.pop() if has_head_norm else None
    x_ref, g_ref, w_ref = refs
    j = pl.program_id(1)

    @pl.when(j == 0)
    def _():
        h_scr[...] = _rms_rows(x_ref[...], g_ref[...]).astype(h_scr.dtype)

    acc = _mm(h_scr[...], w_ref[...], _dot, precise)
    tn = acc.shape[1]

    def head_normed(a):
        parts = []
        for c in range(tn // ATT_HEAD_DIM):
            blk = a[:, c * ATT_HEAD_DIM:(c + 1) * ATT_HEAD_DIM]
            parts.append(_rms_rows(blk, hg_ref[...]))
        return jnp.concatenate(parts, axis=1)

    def emit(a):
        if out_scale != 1.0:
            a = a * out_scale
        o_ref[...] = a.astype(o_ref.dtype)
        if bf16_copy:
            o2_ref[...] = a.astype(BF16)

    if not has_head_norm:
        emit(acc)
    else:
        @pl.when(j < n_normed_tiles)
        def _():
            emit(head_normed(acc))

        @pl.when(j >= n_normed_tiles)
        def _():
            emit(acc)


def norm_matmul(x, g, w, *, tm, tn, head_g=None, n_normed_cols=0, out_scale=1.0, out_dtype=F32,
                bf16_copy=False, precise=False, n_cols=None, name="norm_matmul"):
    m, k = x.shape
    n = w.shape[1] if n_cols is None else n_cols
    assert m % tm == 0 and n % tn == 0 and n_normed_cols % tn == 0
    has_hn = head_g is not None
    in_specs = [pl.BlockSpec((tm, k), lambda i, j: (i, 0)),
                pl.BlockSpec((1, k), lambda i, j: (0, 0)),
                pl.BlockSpec((k, tn), lambda i, j: (0, j))]
    args = [x, g.reshape(1, k), w]
    if has_hn:
        in_specs.append(pl.BlockSpec((1, ATT_HEAD_DIM), lambda i, j: (0, 0)))
        args.append(head_g.reshape(1, ATT_HEAD_DIM))
    out_specs = [pl.BlockSpec((tm, tn), lambda i, j: (i, j))] * 2
    out_shape = [jax.ShapeDtypeStruct((m, n), out_dtype), jax.ShapeDtypeStruct((m, n), BF16)]
    return pl.pallas_call(
        functools.partial(_norm_matmul_kernel, n_normed_tiles=n_normed_cols // tn,
                          has_head_norm=has_hn, out_scale=out_scale, bf16_copy=bf16_copy,
                          precise=precise),
        grid=(m // tm, n // tn),
        in_specs=in_specs,
        out_specs=out_specs if bf16_copy else out_specs[0],
        out_shape=out_shape if bf16_copy else out_shape[0],
        scratch_shapes=[pltpu.VMEM((tm, k), F32 if precise else BF16)],
        compiler_params=_params("parallel", "arbitrary"),
        name=name,
    )(*args)


def _matmul_res_kernel(a_ref, w_ref, r_ref, o_ref, *, precise):
    o_ref[...] = r_ref[...] + _mm(a_ref[...], w_ref[...], _dot, precise)


def matmul_res(a, w, res, *, tm, tn, precise=False, name="matmul_res"):
    m, k = a.shape
    n = w.shape[1]
    assert m % tm == 0 and n % tn == 0
    return pl.pallas_call(
        functools.partial(_matmul_res_kernel, precise=precise),
        grid=(m // tm, n // tn),
        in_specs=[pl.BlockSpec((tm, k), lambda i, j: (i, 0)),
                  pl.BlockSpec((k, tn), lambda i, j: (0, j)),
                  pl.BlockSpec((tm, tn), lambda i, j: (i, j))],
        out_specs=pl.BlockSpec((tm, tn), lambda i, j: (i, j)),
        out_shape=jax.ShapeDtypeStruct((m, n), F32),
        compiler_params=_params("parallel", "arbitrary"),
        name=name,
    )(a, w, res)


def _ssd_kernel(z_ref, xs_ref, b_ref, c_ref, dt_ref, conv0_ref, ssm0_ref, cw_ref, cbias_ref,
                dtb_ref, alog_ref, dskip_ref, gn_ref, expand_ref, tril_ref, eye_ref,
                yn_ref, ssm_out_ref, conv_out_ref,
                ext_scr, act_scr, y_scr, h_scr, *, chunk, n_heads, d_inner, gn_cols, precise):
    q = chunk
    c = pl.program_id(1)
    nc = pl.num_programs(1)
    pad = SUBLANES
    heads_per_group = n_heads // SSM_GROUPS
    gcols = heads_per_group * SSM_HEAD_DIM

    @pl.when(c == 0)
    def _():
        ext_scr[0:pad, :] = conv0_ref[0]
        h_scr[...] = ssm0_ref[0]

    ext_scr[pad:pad + q, 0:d_inner] = xs_ref[...]
    ext_scr[pad:pad + q, d_inner:d_inner + gn_cols] = b_ref[...]
    ext_scr[pad:pad + q, d_inner + gn_cols:] = c_ref[...]

    acc = jnp.broadcast_to(cbias_ref[...], (q, ext_scr.shape[1]))
    for tap in range(CONV_WIDTH):
        start = pad - (CONV_WIDTH - 1) + tap
        acc = acc + ext_scr[start:start + q, :] * cw_ref[tap:tap + 1, :]
    act_scr[...] = _silu(acc)

    tail = ext_scr[q:q + pad, :]
    conv_out_ref[0] = tail
    ext_scr[0:pad, :] = tail

    dt = _softplus(dt_ref[...] + dtb_ref[...])
    a = dt * (-jnp.exp(alog_ref[...]))
    tril = tril_ref[...]
    eye = eye_ref[...]
    a_cum = sum(_dot(tril, t) for t in _split_bf16(a, 3))
    a_cum_t = sum(_dot_nt(eye, t) for t in _split_bf16(a_cum, 3))
    dt_t = sum(_dot_nt(eye, t) for t in _split_bf16(dt, 3))
    a_last = a_cum[q - 1:q, :]
    w_state = dt * jnp.exp(a_last - a_cum)
    e_cum = jnp.exp(a_cum)
    expand = expand_ref[...]
    w_state_x = sum(_dot(t, expand) for t in _split_bf16(w_state, 2))
    e_cum_x = sum(_dot(t, expand) for t in _split_bf16(e_cum, 2))

    row_i = lax.broadcasted_iota(jnp.int32, (q, q), 0)
    col_i = lax.broadcasted_iota(jnp.int32, (q, q), 1)
    causal = col_i <= row_i
    lane = lax.broadcasted_iota(jnp.int32, (q, LANES), 1)
    lo_half = lane < SSM_HEAD_DIM

    state_rows_are_heads = h_scr.shape[0] == d_inner
    ones_state = jnp.ones((q, SSM_STATE), BF16)
    operand = (lambda v: v) if precise else (lambda v: v.astype(BF16))
    for g in range(SSM_GROUPS):
        bg = operand(act_scr[:, d_inner + g * SSM_STATE:d_inner + (g + 1) * SSM_STATE])
        cg = operand(act_scr[:, d_inner + gn_cols + g * SSM_STATE:
                             d_inner + gn_cols + (g + 1) * SSM_STATE])
        cb = _mm(cg, bg, _dot_nt, precise)
        for j in range(gcols // LANES):
            c0 = g * gcols + j * LANES
            xs_blk = act_scr[:, c0:c0 + LANES]
            y_pair = None
            for half in range(2):
                h = (c0 // SSM_HEAD_DIM) + half
                seg = a_cum[:, h:h + 1] - a_cum_t[h:h + 1, :]
                m = cb * jnp.exp(jnp.where(causal, seg, -jnp.inf)) * dt_t[h:h + 1, :]
                keep = lo_half if half == 0 else jnp.logical_not(lo_half)
                part = _mm(m, jnp.where(keep, xs_blk, 0.0), _dot, precise)
                y_pair = part if y_pair is None else y_pair + part
            y_scr[:, c0:c0 + LANES] = y_pair
        gs = slice(g * gcols, (g + 1) * gcols)
        xw = act_scr[:, gs] * w_state_x[:, gs]
        if state_rows_are_heads:
            h_g = h_scr[gs, :]
            y_off = _mm(cg, h_g, _dot_nt, precise) * e_cum_x[:, gs]
            last = jnp.where(row_i[:, 0:1] == q - 1, e_cum_x[:, gs], 0.0)
            decay = sum(_dot_tn(t, ones_state) for t in _split_bf16(last, 3))
            h_scr[gs, :] = h_g * decay + _mm(xw, bg, _dot_tn, precise)
        else:
            h_g = h_scr[:, gs]
            y_off = _mm(cg, h_g, _dot, precise) * e_cum_x[:, gs]
            h_scr[:, gs] = h_g * e_cum_x[q - 1:q, gs] + _mm(bg, xw, _dot_tn, precise)
        y_scr[:, gs] = y_scr[:, gs] + y_off

    y = y_scr[...] + act_scr[:, 0:d_inner] * dskip_ref[...]
    y = y * _silu(z_ref[...])
    gw = d_inner // SSM_GROUPS
    parts = []
    for g in range(SSM_GROUPS):
        parts.append(_rms_rows(y[:, g * gw:(g + 1) * gw], gn_ref[:, g * gw:(g + 1) * gw]))
    yn_ref[...] = jnp.concatenate(parts, axis=1).astype(yn_ref.dtype)

    @pl.when(c == nc - 1)
    def _():
        ssm_out_ref[0] = h_scr[...]


def ssd_mixer(proj, dt_raw, conv_state, ssm_state, conv_w, conv_b, dt_bias, a_log, d_skip, gn,
              *, batch, seq, chunk, d_inner, precise=False, name="ssd"):
    n_heads = d_inner // SSM_HEAD_DIM
    gn_cols = SSM_GROUPS * SSM_STATE
    conv_dim = d_inner + 2 * gn_cols
    nc = seq // chunk
    assert seq % chunk == 0 and n_heads <= LANES
    q = chunk
    pad = SUBLANES

    conv0 = jnp.concatenate(
        [jnp.zeros((batch, pad - (CONV_WIDTH - 1), conv_dim), F32), conv_state], axis=1)
    state_as_stored = q < SSM_STATE
    ssm0 = ssm_state.reshape(batch, d_inner, SSM_STATE)
    if not state_as_stored:
        ssm0 = jnp.transpose(ssm0, (0, 2, 1))
    state_shape = ssm0.shape[1:]

    def lane_pad(v):
        return jnp.pad(v, (0, LANES - v.shape[0])).reshape(1, LANES)

    head_of_col = jnp.arange(d_inner) // SSM_HEAD_DIM
    expand = (jnp.arange(LANES)[:, None] == head_of_col[None, :]).astype(BF16)
    tril = jnp.tril(jnp.ones((q, q), BF16))
    eye = jnp.eye(LANES, dtype=BF16)
    dskip_cols = jnp.repeat(d_skip, SSM_HEAD_DIM).reshape(1, d_inner)

    zb = d_inner // d_inner
    row = lambda b, c: b * nc + c
    const2 = lambda b, c: (0, 0)
    out_dtype = BF16 if (q % 16 == 0 and not precise) else F32
    yn, ssm_t, conv_tail = pl.pallas_call(
        functools.partial(_ssd_kernel, chunk=q, n_heads=n_heads, d_inner=d_inner, gn_cols=gn_cols,
                          precise=precise),
        grid=(batch, nc),
        in_specs=[
            pl.BlockSpec((q, d_inner), lambda b, c: (row(b, c), 0)),
            pl.BlockSpec((q, d_inner), lambda b, c: (row(b, c), zb)),
            pl.BlockSpec((q, gn_cols), lambda b, c: (row(b, c), 2 * d_inner // gn_cols)),
            pl.BlockSpec((q, gn_cols), lambda b, c: (row(b, c), 2 * d_inner // gn_cols + 1)),
            pl.BlockSpec((q, LANES), lambda b, c: (row(b, c), 0)),
            pl.BlockSpec((1, pad, conv_dim), lambda b, c: (b, 0, 0)),
            pl.BlockSpec((1,) + state_shape, lambda b, c: (b, 0, 0)),
            pl.BlockSpec((CONV_WIDTH, conv_dim), const2),
            pl.BlockSpec((1, conv_dim), const2),
            pl.BlockSpec((1, LANES), const2),
            pl.BlockSpec((1, LANES), const2),
            pl.BlockSpec((1, d_inner), const2),
            pl.BlockSpec((1, d_inner), const2),
            pl.BlockSpec((LANES, d_inner), const2),
            pl.BlockSpec((q, q), const2),
            pl.BlockSpec((LANES, LANES), const2),
        ],
        out_specs=[
            pl.BlockSpec((q, d_inner), lambda b, c: (row(b, c), 0)),
            pl.BlockSpec((1,) + state_shape, lambda b, c: (b, 0, 0)),
            pl.BlockSpec((1, pad, conv_dim), lambda b, c: (b, 0, 0)),
        ],
        out_shape=[
            jax.ShapeDtypeStruct((batch * seq, d_inner), out_dtype),
            jax.ShapeDtypeStruct((batch,) + state_shape, F32),
            jax.ShapeDtypeStruct((batch, pad, conv_dim), F32),
        ],
        scratch_shapes=[
            pltpu.VMEM((q + pad, conv_dim), F32),
            pltpu.VMEM((q, conv_dim), F32),
            pltpu.VMEM((q, d_inner), F32),
            pltpu.VMEM(state_shape, F32),
        ],
        compiler_params=_params("parallel", "arbitrary"),
        name=name,
    )(proj, proj, proj, proj, dt_raw, conv0, ssm0, conv_w, conv_b.reshape(1, conv_dim),
      lane_pad(dt_bias), lane_pad(a_log), dskip_cols, gn.reshape(1, d_inner), expand, tril, eye)
    if not state_as_stored:
        ssm_t = jnp.transpose(ssm_t, (0, 2, 1))
    ssm_new = ssm_t.reshape(batch, n_heads, SSM_HEAD_DIM, SSM_STATE)
    return yn, ssm_new, conv_tail[:, pad - (CONV_WIDTH - 1):]


def _router_kernel(x_ref, g_ref, whi_ref, wlo_ref, bias_ref, gates_ref, hn_ref):
    h = _rms_rows(x_ref[...], g_ref[...])
    hn_ref[...] = h.astype(BF16)
    h_hi, h_lo = _split_bf16(h, 2)
    logits = (_dot(h_hi, whi_ref[...]) + _dot(h_lo, whi_ref[...]) + _dot(h_hi, wlo_ref[...])
              + bias_ref[...])
    tm = logits.shape[0]
    lane = lax.broadcasted_iota(jnp.int32, (tm, LANES), 1)
    neg = -jnp.inf
    is_group = lane < N_EXPERT_GROUPS
    gl = jnp.where(is_group, logits, neg)
    g_max = jnp.max(gl, axis=-1, keepdims=True)
    g_idx = jnp.min(jnp.where(gl == g_max, lane, LANES), axis=-1, keepdims=True)
    g_w = 1.0 / jnp.sum(jnp.where(is_group, jnp.exp(gl - g_max), 0.0), axis=-1, keepdims=True)
    first = N_EXPERT_GROUPS + g_idx * EXPERTS_PER_GROUP
    in_group = (lane >= first) & (lane < first + EXPERTS_PER_GROUP)
    el = jnp.where(in_group, logits, neg)
    v1 = jnp.max(el, axis=-1, keepdims=True)
    i1 = jnp.min(jnp.where(el == v1, lane, LANES), axis=-1, keepdims=True)
    el2 = jnp.where(lane == i1, neg, el)
    v2 = jnp.max(el2, axis=-1, keepdims=True)
    i2 = jnp.min(jnp.where(el2 == v2, lane, LANES), axis=-1, keepdims=True)
    e2 = jnp.exp(v2 - v1)
    w1 = g_w / (1.0 + e2)
    w2 = g_w * e2 / (1.0 + e2)
    gates = jnp.where(lane == i1, w1, 0.0) + jnp.where(lane == i2, w2, 0.0)
    gates = pltpu.roll(gates, LANES - N_EXPERT_GROUPS, 1)
    gates_ref[...] = jnp.where(lane == N_EXPERTS, g_idx.astype(F32), gates)


def moe_router(x, g, w_group, b_group, w_expert, b_expert, *, tm, name="router"):
    m, k = x.shape
    w = jnp.concatenate([w_group, w_expert.reshape(k, N_EXPERTS)], axis=1)
    w = jnp.pad(w, ((0, 0), (0, LANES - w.shape[1])))
    w_hi = w.astype(BF16)
    w_lo = (w - w_hi.astype(F32)).astype(BF16)
    bias = jnp.pad(jnp.concatenate([b_group, b_expert.reshape(N_EXPERTS)]),
                   (0, LANES - N_EXPERT_GROUPS - N_EXPERTS)).reshape(1, LANES)
    return pl.pallas_call(
        _router_kernel,
        grid=(m // tm,),
        in_specs=[pl.BlockSpec((tm, k), lambda i: (i, 0)),
                  pl.BlockSpec((1, k), lambda i: (0, 0)),
                  pl.BlockSpec((k, LANES), lambda i: (0, 0)),
                  pl.BlockSpec((k, LANES), lambda i: (0, 0)),
                  pl.BlockSpec((1, LANES), lambda i: (0, 0))],
        out_specs=[pl.BlockSpec((tm, LANES), lambda i: (i, 0)),
                   pl.BlockSpec((tm, k), lambda i: (i, 0))],
        out_shape=[jax.ShapeDtypeStruct((m, LANES), F32),
                   jax.ShapeDtypeStruct((m, k), BF16)],
        compiler_params=_params("parallel"),
        name=name,
    )(x, g.reshape(1, k), w_hi, w_lo, bias)


MOE_SUB_ROWS = 128


def _moe_kernel(hn_ref, gates_ref, x_ref, wg_ref, wu_ref, wd_ref, lower_ref, upper_ref, eye_ref,
                o_ref, hs_scr, gs_scr, acc_scr, perm_t_scr, bounds_scr):
    e = pl.program_id(1)
    n_e = pl.num_programs(1)
    tm = hn_ref.shape[0]
    lane = lax.broadcasted_iota(jnp.int32, (tm, LANES), 1)

    @pl.when(e == 0)
    def _():
        gates = gates_ref[...]
        grp = gates[:, N_EXPERTS:N_EXPERTS + 1]
        onehot = jnp.where((lane.astype(F32) == grp) & (lane < N_EXPERT_GROUPS), 1.0, 0.0)
        counts = jnp.sum(onehot, axis=0, keepdims=True)
        starts = sum(_dot(t, upper_ref[...]) for t in _split_bf16(counts, 3))
        rank = _dot(lower_ref[...], onehot.astype(BF16))
        pos = jnp.sum(onehot * (starts + rank), axis=-1, keepdims=True)
        pos_rows = sum(_dot_nt(eye_ref[...], t)
                       for t in _split_bf16(jnp.broadcast_to(pos, (tm, LANES)), 3))
        r_i = lax.broadcasted_iota(jnp.int32, (tm, tm), 0).astype(F32)
        c_i = lax.broadcasted_iota(jnp.int32, (tm, tm), 1).astype(F32)
        perm = jnp.where(r_i == pos_rows[0:1, :], 1.0, 0.0).astype(BF16)
        perm_t_scr[...] = jnp.where(c_i == pos, 1.0, 0.0).astype(BF16)
        hs_scr[...] = _dot(perm, hn_ref[...]).astype(BF16)
        gs_scr[...] = sum(_dot(perm, t) for t in _split_bf16(gates, 3))
        acc_scr[...] = jnp.zeros_like(acc_scr)
        ends = starts + counts
        for g in range(N_EXPERT_GROUPS):
            bounds_scr[g] = jnp.sum(jnp.where(lane[0:1] == g, starts, 0.0)).astype(jnp.int32)
            bounds_scr[N_EXPERT_GROUPS + g] = jnp.sum(
                jnp.where(lane[0:1] == g, ends, 0.0)).astype(jnp.int32)

    g_e = e // EXPERTS_PER_GROUP
    row_lo = bounds_scr[g_e]
    row_hi = bounds_scr[N_EXPERT_GROUPS + g_e]
    sub = min(MOE_SUB_ROWS, tm)
    sub_lane = lax.broadcasted_iota(jnp.int32, (sub, LANES), 1)
    for s in range(tm // sub):
        r0 = s * sub

        @pl.when((r0 < row_hi) & (r0 + sub > row_lo))
        def _():
            h = hs_scr[r0:r0 + sub, :]
            gate = jnp.sum(jnp.where(sub_lane == e, gs_scr[r0:r0 + sub, :], 0.0),
                           axis=-1, keepdims=True)
            hidden = _silu(_dot(h, wg_ref[0])) * _dot(h, wu_ref[0]) * gate
            acc_scr[r0:r0 + sub, :] += _dot(hidden.astype(BF16), wd_ref[0])

    @pl.when(e == n_e - 1)
    def _():
        y = sum(_dot(perm_t_scr[...], t) for t in _split_bf16(acc_scr[...], 2))
        o_ref[...] = x_ref[...] + y


def moe_ffn(hn, gates, x, w_gate, w_up, w_down, *, tm, name="moe"):
    m, k = hn.shape
    n_e, _, f = w_gate.shape
    assert tm % min(MOE_SUB_ROWS, tm) == 0 and n_e == N_EXPERTS
    lower = jnp.tril(jnp.ones((tm, tm), BF16), k=-1)
    upper = jnp.triu(jnp.ones((LANES, LANES), BF16), k=1)
    eye = jnp.eye(LANES, dtype=BF16)
    const = lambda i, e: (0, 0)
    return pl.pallas_call(
        _moe_kernel,
        grid=(m // tm, n_e),
        in_specs=[pl.BlockSpec((tm, k), lambda i, e: (i, 0)),
                  pl.BlockSpec((tm, LANES), lambda i, e: (i, 0)),
                  pl.BlockSpec((tm, k), lambda i, e: (i, 0)),
                  pl.BlockSpec((1, k, f), lambda i, e: (e, 0, 0)),
                  pl.BlockSpec((1, k, f), lambda i, e: (e, 0, 0)),
                  pl.BlockSpec((1, f, k), lambda i, e: (e, 0, 0)),
                  pl.BlockSpec((tm, tm), const),
                  pl.BlockSpec((LANES, LANES), const),
                  pl.BlockSpec((LANES, LANES), const)],
        out_specs=pl.BlockSpec((tm, k), lambda i, e: (i, 0)),
        out_shape=jax.ShapeDtypeStruct((m, k), F32),
        scratch_shapes=[pltpu.VMEM((tm, k), BF16), pltpu.VMEM((tm, LANES), F32),
                        pltpu.VMEM((tm, k), F32), pltpu.VMEM((tm, tm), BF16),
                        pltpu.SMEM((2 * N_EXPERT_GROUPS,), jnp.int32)],
        compiler_params=_params("parallel", "arbitrary"),
        name=name,
    )(hn, gates, x, w_gate, w_up, w_down, lower, upper, eye)


def _logf_kernel(x_ref, g_ref, w_ref, b_ref, *rest, n_heads, with_bias_cols):
    h = _rms_rows(x_ref[...], g_ref[...]).astype(BF16)
    zf = _dot(h, w_ref[...]) + b_ref[...]
    logf = -_softplus(-zf)
    if not with_bias_cols:
        (logf_ref,) = rest
        logf_ref[...] = logf
        return
    tril_ref, logf_ref, bias_ref, carry_scr = rest
    logf_ref[...] = logf

    @pl.when(pl.program_id(0) == 0)
    def _():
        carry_scr[...] = jnp.zeros_like(carry_scr)

    cum = sum(_dot(tril_ref[...], t) for t in _split_bf16(logf, 3)) + carry_scr[...]
    tm = cum.shape[0]
    carry_scr[...] = cum[tm - 1:tm, :]
    lane = lax.broadcasted_iota(jnp.int32, (tm, LANES), 1)
    for hd in range(n_heads):
        t0, t1, t2 = [t.astype(F32) for t in _split_bf16(cum[:, hd:hd + 1] * (-LOG2E), 3)]
        blk = jnp.where(lane == 0, t0, jnp.where(lane == 1, t1, jnp.where(lane == 2, t2, 0.0)))
        bias_ref[:, hd * LANES:(hd + 1) * LANES] = blk.astype(BF16)


def logf_proj(x, g, w_f, b_f, *, tm, with_bias_cols, name="logf"):
    m, k = x.shape
    n_h = w_f.shape[1]
    w = jnp.pad(w_f, ((0, 0), (0, LANES - n_h))).astype(BF16)
    b = jnp.pad(b_f, (0, LANES - n_h)).reshape(1, LANES)
    in_specs = [pl.BlockSpec((tm, k), lambda i: (i, 0)),
                pl.BlockSpec((1, k), lambda i: (0, 0)),
                pl.BlockSpec((k, LANES), lambda i: (0, 0)),
                pl.BlockSpec((1, LANES), lambda i: (0, 0))]
    args = [x, g.reshape(1, k), w, b]
    out_specs = [pl.BlockSpec((tm, LANES), lambda i: (i, 0))]
    out_shape = [jax.ShapeDtypeStruct((m, LANES), F32)]
    scratch = []
    if with_bias_cols:
        in_specs.append(pl.BlockSpec((tm, tm), lambda i: (0, 0)))
        args.append(jnp.tril(jnp.ones((tm, tm), BF16)))
        out_specs.append(pl.BlockSpec((tm, n_h * LANES), lambda i: (i, 0)))
        out_shape.append(jax.ShapeDtypeStruct((m, n_h * LANES), BF16))
        scratch.append(pltpu.VMEM((1, LANES), F32))
    return pl.pallas_call(
        functools.partial(_logf_kernel, n_heads=n_h, with_bias_cols=with_bias_cols),
        grid=(m // tm,),
        in_specs=in_specs,
        out_specs=out_specs,
        out_shape=out_shape,
        scratch_shapes=scratch,
        compiler_params=_params("arbitrary"),
        name=name,
    )(*args)


def _flash_kernel(q_ref, k_ref, v_ref, bias_ref, o_ref, vt_scr, m_scr, l_scr, acc_scr, *,
                  tq, heads_per_step):
    qi = pl.program_id(1)
    d = ATT_HEAD_DIM
    n_blocks = k_ref.shape[0] // tq

    @pl.when(qi == 0)
    def _():
        for hh in range(heads_per_step):
            for c in range(n_blocks):
                blk = v_ref[c * tq:(c + 1) * tq, hh * d:(hh + 1) * d].astype(F32)
                vt_scr[hh, c] = blk.T.astype(BF16)

    lane = lax.broadcasted_iota(jnp.int32, (tq, d), 1)
    ones_cols = jnp.where(lane < 3, 1.0, 0.0).astype(BF16)
    q_aug = [jnp.concatenate([q_ref[:, hh * d:(hh + 1) * d], ones_cols], axis=1)
             for hh in range(heads_per_step)]

    def block(ki, n_sub, masked, first):
        for hh in range(heads_per_step):
            scores = []
            for u in range(n_sub):
                start = pl.multiple_of((ki + u) * tq, tq)
                k_aug = jnp.concatenate([k_ref[pl.ds(start, tq), hh * d:(hh + 1) * d],
                                         bias_ref[pl.ds(start, tq), hh * d:(hh + 1) * d]], axis=1)
                s = _dot_nt(k_aug, q_aug[hh])
                if masked:
                    key_i = lax.broadcasted_iota(jnp.int32, (tq, tq), 0)
                    qry_i = lax.broadcasted_iota(jnp.int32, (tq, tq), 1)
                    s = jnp.where(key_i <= qry_i, s, -jnp.inf)
                scores.append(s)
            s_max = functools.reduce(jnp.maximum,
                                     [jnp.max(s, axis=0, keepdims=True) for s in scores])
            if first:
                m_new = s_max
            else:
                m_prev = m_scr[hh]
                m_new = jnp.maximum(m_prev, s_max)
                alpha = jnp.exp2(m_prev - m_new)
            probs = [jnp.exp2(s - m_new) for s in scores]
            l_new = sum(jnp.sum(p, axis=0, keepdims=True) for p in probs)
            pv = sum(_dot(vt_scr[hh, ki + u], p.astype(BF16)) for u, p in enumerate(probs))
            if first:
                l_scr[hh] = l_new
                acc_scr[hh] = pv
            else:
                l_scr[hh] = l_scr[hh] * alpha + l_new
                acc_scr[hh] = acc_scr[hh] * alpha + pv
            m_scr[hh] = m_new

    def pair_body(kp, carry):
        block(2 * kp, 2, False, False)
        return carry

    block(qi, 1, True, True)
    lax.fori_loop(0, qi // 2, pair_body, 0)

    @pl.when(qi % 2 == 1)
    def _():
        block(qi - 1, 1, False, False)
    for hh in range(heads_per_step):
        out_t = acc_scr[hh] / l_scr[hh]
        o_ref[:, hh * d:(hh + 1) * d] = out_t.T.astype(o_ref.dtype)


def flash_attention(q, kv, bias_cols, *, n_heads, tq, heads_per_step, name="flash"):
    seq = q.shape[0]
    d = ATT_HEAD_DIM
    w = heads_per_step * d
    n_steps = n_heads // heads_per_step
    assert n_heads % heads_per_step == 0 and seq % tq == 0
    return pl.pallas_call(
        functools.partial(_flash_kernel, tq=tq, heads_per_step=heads_per_step),
        grid=(n_steps, seq // tq),
        in_specs=[pl.BlockSpec((tq, w), lambda h, i: (i, h)),
                  pl.BlockSpec((seq, w), lambda h, i: (0, h)),
                  pl.BlockSpec((seq, w), lambda h, i: (0, n_steps + h)),
                  pl.BlockSpec((seq, w), lambda h, i: (0, h))],
        out_specs=pl.BlockSpec((tq, w), lambda h, i: (i, h)),
        out_shape=jax.ShapeDtypeStruct((seq, n_heads * d), BF16),
        scratch_shapes=[pltpu.VMEM((heads_per_step, seq // tq, d, tq), BF16),
                        pltpu.VMEM((heads_per_step, 1, tq), F32),
                        pltpu.VMEM((heads_per_step, 1, tq), F32),
                        pltpu.VMEM((heads_per_step, d, tq), F32)],
        compiler_params=_params("parallel", "arbitrary"),
        name=name,
    )(q, kv, kv, bias_cols)


def _page_sums_kernel(lf_ref, upper_ref, ones_ref, inner_ref, total_ref):
    terms = _split_bf16(lf_ref[...], 3)
    inner_ref[...] = sum(_dot(upper_ref[...], t) for t in terms) * LOG2E
    total_ref[...] = sum(_dot(ones_ref[...], t) for t in terms) * LOG2E


def page_forget_sums(cache_logf, name="page_sums"):
    n_pool, page, n_heads = cache_logf.shape
    cols = n_pool * n_heads
    lf_t = jnp.transpose(cache_logf, (1, 0, 2)).reshape(page, cols)
    tn = max([t for t in range(LANES, 4096 + 1, LANES) if cols % t == 0], default=cols)
    return pl.pallas_call(
        _page_sums_kernel,
        grid=(cols // tn,),
        in_specs=[pl.BlockSpec((page, tn), lambda i: (0, i)),
                  pl.BlockSpec((page, page), lambda i: (0, 0)),
                  pl.BlockSpec((SUBLANES, page), lambda i: (0, 0))],
        out_specs=[pl.BlockSpec((page, tn), lambda i: (0, i)),
                   pl.BlockSpec((SUBLANES, tn), lambda i: (0, i))],
        out_shape=[jax.ShapeDtypeStruct((page, cols), F32),
                   jax.ShapeDtypeStruct((SUBLANES, cols), F32)],
        compiler_params=_params("parallel"),
        name=name,
    )(lf_t, jnp.triu(jnp.ones((page, page), BF16), k=1), jnp.ones((SUBLANES, page), BF16))


def _new_token_bias_kernel(lf_ref, tril_ref, o_ref):
    o_ref[...] = sum(_dot(tril_ref[...], t) for t in _split_bf16(lf_ref[...], 3)) * (-LOG2E)


def new_token_bias(logf_new, name="new_token_bias"):
    n_b, t_pad, n_heads = logf_new.shape
    rows = n_b * t_pad
    r = jnp.arange(rows)
    same_request = (r[:, None] // t_pad) == (r[None, :] // t_pad)
    tril = (same_request & (r[None, :] <= r[:, None])).astype(BF16)
    out = pl.pallas_call(
        _new_token_bias_kernel,
        out_shape=jax.ShapeDtypeStruct((rows, n_heads), F32),
        compiler_params=pltpu.CompilerParams(vmem_limit_bytes=VMEM_LIMIT),
        name=name,
    )(logf_new.reshape(rows, n_heads), tril)
    return out.reshape(n_b, t_pad, n_heads)


def _paged_attn_kernel(pt_ref, q_ref, kvn_ref, bn_ref, mask_new_ref, mask_page_ref, *rest,
                       n_heads, pages_per_step):
    kv_refs = rest[:pages_per_step]
    inner_refs = rest[pages_per_step:2 * pages_per_step]
    total_refs = rest[2 * pages_per_step:3 * pages_per_step]
    o_ref, m_scr, l_scr, acc_scr, carry_scr = rest[3 * pages_per_step:]
    del pt_ref
    j = pl.program_id(1)
    n_steps = pl.num_programs(1)
    n_half = n_heads // SUBLANES

    def rows_of(kv_view, first_head):
        n_keys = kv_view.shape[0]
        blk = kv_view[:, first_head:first_head + SUBLANES, :]
        return blk.reshape(n_keys * SUBLANES, ATT_HEAD_DIM).astype(BF16)

    def attend(kv_views, half, bias_rows, mask, first):
        lo = half * SUBLANES
        scores = [_dot_nt(q_ref[0, half], rows_of(view, lo)) + mask + bias
                  for view, bias in zip(kv_views, bias_rows)]
        s_max = functools.reduce(jnp.maximum, [jnp.max(s, axis=-1, keepdims=True) for s in scores])
        if first:
            m_new = s_max
        else:
            m_prev = m_scr[half]
            m_new = jnp.maximum(m_prev, s_max)
            alpha = jnp.exp2(m_prev - m_new)
        probs = [jnp.exp2(s - m_new) for s in scores]
        l_new = sum(jnp.sum(p, axis=-1, keepdims=True) for p in probs)
        pv = sum(_dot(p.astype(BF16), rows_of(view, n_heads + lo))
                 for p, view in zip(probs, kv_views))
        if first:
            l_scr[half] = l_new
            acc_scr[half] = pv
        else:
            l_scr[half] = l_scr[half] * alpha + l_new
            acc_scr[half] = acc_scr[half] * alpha + pv
        m_scr[half] = m_new

    @pl.when(j == 0)
    def _():
        for half in range(n_half):
            attend([kvn_ref.at[0]], half, [bn_ref[0, half]], mask_new_ref[...], True)
        carry_scr[...] = jnp.zeros_like(carry_scr)

    for half in range(n_half):
        carry = carry_scr[half]
        biases = []
        for inner, total in zip(inner_refs, total_refs):
            biases.append(inner[0, half] + carry)
            carry = carry + total[0, half]
        carry_scr[half] = carry
        attend([r.at[0] for r in kv_refs], half, biases, mask_page_ref[...], False)

    @pl.when(j == n_steps - 1)
    def _():
        for half in range(n_half):
            o_ref[0, half] = acc_scr[half] / l_scr[half]


def paged_attention(q, kv_new, logf_new, cache_kv, cache_logf, page_table, *, pages_per_step,
                    name="paged_attn"):
    n_b, n_tok, hd = q.shape
    d = ATT_HEAD_DIM
    n_heads = hd // d
    n_half = n_heads // SUBLANES
    n_pool, page = cache_logf.shape[0], cache_logf.shape[1]
    n_pages = page_table.shape[1]
    rows = SUBLANES * n_tok
    t_pad = -(-n_tok // 16) * 16
    assert n_pages % pages_per_step == 0 and n_heads % SUBLANES == 0 and rows % 16 == 0
    n_steps = n_pages // pages_per_step

    inner, total = page_forget_sums(cache_logf)
    inner_rows = inner.reshape(page, n_pool, n_half, SUBLANES).transpose(1, 2, 0, 3)
    inner_rows = inner_rows.reshape(n_pool, n_half, 1, page * SUBLANES)
    total_rows = jnp.broadcast_to(total[0].reshape(n_pool, n_half, 1, SUBLANES),
                                  (n_pool, n_half, page, SUBLANES))
    total_rows = total_rows.reshape(n_pool, n_half, 1, page * SUBLANES)
    neg_c_new = new_token_bias(jnp.pad(logf_new, ((0, 0), (0, t_pad - n_tok), (0, 0))))
    bias_new = neg_c_new.reshape(n_b, t_pad, n_half, SUBLANES).transpose(0, 2, 1, 3)
    bias_new = bias_new.reshape(n_b, n_half, 1, t_pad * SUBLANES)

    row_head = jnp.arange(rows)[:, None] // n_tok
    row_tok = jnp.arange(rows)[:, None] % n_tok

    def mask_for(n_keys, causal):
        col = jnp.arange(n_keys * SUBLANES)[None, :]
        ok = (col % SUBLANES) == row_head
        if causal:
            ok = ok & ((col // SUBLANES) <= row_tok)
        return jnp.where(ok, 0.0, -jnp.inf).astype(F32)

    q_rows = q.reshape(n_b, n_tok, n_half, SUBLANES, d).transpose(0, 2, 3, 1, 4)
    q_rows = q_rows.reshape(n_b, n_half, rows, d).astype(BF16)
    cache4 = cache_kv.reshape(n_pool, page, 2 * n_heads, d)
    kvn = jnp.pad(kv_new.reshape(n_b, n_tok, 2 * n_heads, d),
                  ((0, 0), (0, t_pad - n_tok), (0, 0), (0, 0)))

    def page_pos(j, i):
        return n_pages - 1 - (j * pages_per_step + i)

    in_specs = [pl.BlockSpec((1, n_half, rows, d), lambda b, j, pt: (b, 0, 0, 0)),
                pl.BlockSpec((1, t_pad, 2 * n_heads, d), lambda b, j, pt: (b, 0, 0, 0)),
                pl.BlockSpec((1, n_half, 1, t_pad * SUBLANES), lambda b, j, pt: (b, 0, 0, 0)),
                pl.BlockSpec((rows, t_pad * SUBLANES), lambda b, j, pt: (0, 0)),
                pl.BlockSpec((rows, page * SUBLANES), lambda b, j, pt: (0, 0))]
    in_specs += [pl.BlockSpec((1, page, 2 * n_heads, d),
                              functools.partial(lambda b, j, pt, i: (pt[b, page_pos(j, i)], 0, 0, 0), i=i))
                 for i in range(pages_per_step)]
    sums_specs = [pl.BlockSpec((1, n_half, 1, page * SUBLANES),
                               functools.partial(lambda b, j, pt, i: (pt[b, page_pos(j, i)], 0, 0, 0), i=i))
                  for i in range(pages_per_step)]
    in_specs += sums_specs + sums_specs
    grid_spec = pltpu.PrefetchScalarGridSpec(
        num_scalar_prefetch=1,
        grid=(n_b, n_steps),
        in_specs=in_specs,
        out_specs=pl.BlockSpec((1, n_half, rows, d), lambda b, j, pt: (b, 0, 0, 0)),
        scratch_shapes=[pltpu.VMEM((n_half, rows, 1), F32), pltpu.VMEM((n_half, rows, 1), F32),
                        pltpu.VMEM((n_half, rows, d), F32),
                        pltpu.VMEM((n_half, 1, page * SUBLANES), F32)])
    o = pl.pallas_call(
        functools.partial(_paged_attn_kernel, n_heads=n_heads, pages_per_step=pages_per_step),
        grid_spec=grid_spec,
        out_shape=jax.ShapeDtypeStruct((n_b, n_half, rows, d), F32),
        compiler_params=_params("parallel", "arbitrary"),
        name=name,
    )(page_table, q_rows, kvn, bias_new, mask_for(t_pad, True), mask_for(page, False),
      *([cache4] * pages_per_step), *([inner_rows] * pages_per_step),
      *([total_rows] * pages_per_step))
    o = o.reshape(n_b, n_half, SUBLANES, n_tok, d).transpose(0, 3, 1, 2, 4)
    return o.reshape(n_b, n_tok, hd)


def _prep_weights(p):
    d_model = p["a_w_in"].shape[1]
    w_in = p["a_w_in"][0]
    n_in = w_in.shape[1]
    d_inner = p["a_w_out"].shape[1]
    n_main = 2 * d_inner + 2 * SSM_GROUPS * SSM_STATE
    w_dt_f32 = jnp.pad(w_in[:, n_main:], ((0, 0), (0, LANES - (n_in - n_main))))
    hd = p["b_w_q"].shape[2]
    return dict(
        w_in=w_in.astype(BF16), w_in_f32=w_in, w_dt=w_dt_f32.astype(BF16), w_dt_f32=w_dt_f32,
        n_main=n_main, w_out=p["a_w_out"][0].astype(BF16),
        w_kv=p["w_kv"][:, :2 * hd].astype(BF16), w_f=p["w_kv"][:, 2 * hd:],
        w_q=p["b_w_q"][0].astype(BF16), w_o=p["b_w_o"][0].astype(BF16),
        m_gate=p["m_w_gate"].astype(BF16), m_up=p["m_w_up"].astype(BF16),
        m_down=p["m_w_down"].astype(BF16), d_model=d_model, d_inner=d_inner, hd=hd)


def _moe_layer(x, p, w, layer, tm):
    gates, hn = moe_router(x, p["norm_ffn"][layer], p["m_w_group"][layer], p["m_b_group"][layer],
                           p["m_w_expert"][layer], p["m_b_expert"][layer], tm=tm,
                           name=f"router{layer}")
    return moe_ffn(hn, gates, x, w["m_gate"][layer], w["m_up"][layer], w["m_down"][layer], tm=tm,
                   name=f"moe{layer}")


def _trunk(x, conv_state, ssm_state, p, w, *, batch, seq, tm, attend):
    d_inner, hd = w["d_inner"], w["hd"]
    n_heads = hd // ATT_HEAD_DIM
    prompt = isinstance(attend, str)
    precise = not prompt
    m_rows = batch * seq
    tm_mm = 2 * tm if (prompt and m_rows % (2 * tm) == 0) else tm
    tn_mm = 1024 if prompt else 512
    proj = norm_matmul(x, p["norm_mix"][0], w["w_in_f32"] if precise else w["w_in"],
                       n_cols=w["n_main"], tm=tm_mm, tn=tn_mm, precise=precise, name="in_proj")
    dt_raw = norm_matmul(x, p["norm_mix"][0], w["w_dt_f32"] if precise else w["w_dt"],
                         tm=tm_mm, tn=LANES, precise=precise, name="dt_proj")
    yn, ssm_new, conv_new = ssd_mixer(
        proj, dt_raw, conv_state, ssm_state, p["a_conv_w"][0], p["a_conv_b"][0], p["a_dt_bias"][0],
        p["a_a_log"][0], p["a_d_skip"][0], p["a_norm"][0],
        batch=batch, seq=seq, chunk=min(128, seq), d_inner=d_inner, precise=precise)
    x = matmul_res(yn, p["a_w_out"][0] if precise else w["w_out"], x, tm=tm_mm,
                   tn=256 if precise else 512, precise=precise, name="out_proj")
    x = _moe_layer(x, p, w, 0, tm)

    kv_out = norm_matmul(x, p["kv_norm"], w["w_kv"], tm=tm_mm, tn=tn_mm, head_g=p["k_norm"],
                         n_normed_cols=hd, bf16_copy=prompt, name="kv_proj")
    logf_out = logf_proj(x, p["kv_norm"], w["w_f"], p["b_forget"], tm=tm, with_bias_cols=prompt)
    logf = logf_out[0]
    q = norm_matmul(x, p["norm_mix"][1], w["w_q"], tm=tm_mm, tn=tn_mm, head_g=p["b_q_norm"][0],
                    n_normed_cols=hd, out_scale=ATT_HEAD_DIM ** -0.5 * LOG2E,
                    out_dtype=BF16 if prompt else F32, name="q_proj")
    if prompt:
        kv, kv_bf = kv_out
        o = flash_attention(q, kv_bf, logf_out[1], n_heads=n_heads, tq=512, heads_per_step=2)
    else:
        kv = kv_out
        cache_kv, cache_logf, page_table = attend
        o = paged_attention(q.reshape(batch, seq, hd), kv.reshape(batch, seq, 2 * hd),
                            logf[:, :n_heads].reshape(batch, seq, n_heads), cache_kv, cache_logf,
                            page_table, pages_per_step=4).reshape(batch * seq, hd)
    x = matmul_res(o, w["w_o"], x, tm=tm_mm, tn=tn_mm, name="o_proj")
    x = _moe_layer(x, p, w, 1, tm)
    return x, ssm_new, conv_new, kv, logf[:, :n_heads]


def kernel(x_prompt, x_sample, state_ssm, state_conv, cache_kv, cache_logf, page_table, norm_mix, norm_ffn, a_w_in, a_conv_w, a_conv_b, a_dt_bias, a_a_log, a_d_skip, a_norm, a_w_out, kv_norm, w_kv, b_forget, k_norm, b_w_q, b_q_norm, b_w_o, m_w_group, m_b_group, m_w_expert, m_b_expert, m_w_gate, m_w_up, m_w_down):
    p = dict(norm_mix=norm_mix, norm_ffn=norm_ffn, a_w_in=a_w_in, a_conv_w=a_conv_w,
             a_conv_b=a_conv_b, a_dt_bias=a_dt_bias, a_a_log=a_a_log, a_d_skip=a_d_skip,
             a_norm=a_norm, a_w_out=a_w_out, kv_norm=kv_norm, w_kv=w_kv, b_forget=b_forget,
             k_norm=k_norm, b_w_q=b_w_q, b_q_norm=b_q_norm, b_w_o=b_w_o, m_w_group=m_w_group,
             m_b_group=m_b_group, m_w_expert=m_w_expert, m_b_expert=m_b_expert,
             m_w_gate=m_w_gate, m_w_up=m_w_up, m_w_down=m_w_down)
    assert a_w_in.shape[0] == 1 and b_w_q.shape[0] == 1, "one SSD layer then one attention layer"
    w = _prep_weights(p)
    d_model, d_inner, hd = w["d_model"], w["d_inner"], w["hd"]
    n_heads = hd // ATT_HEAD_DIM
    conv_dim = state_conv.shape[-1]
    ssm_heads = state_ssm.shape[2]

    b_p, seq_p, _ = x_prompt.shape
    assert b_p == 1, "the prompt path carries one cumulative forget sum over its rows"
    conv0 = jnp.zeros((b_p, CONV_WIDTH - 1, conv_dim), F32)
    ssm0 = jnp.zeros((b_p, ssm_heads, SSM_HEAD_DIM, SSM_STATE), F32)
    y_p, ssm_p, conv_p, kv_p, logf_p = _trunk(
        x_prompt.reshape(b_p * seq_p, d_model), conv0, ssm0, p, w,
        batch=b_p, seq=seq_p, tm=512, attend="prompt")

    b_s, seq_s, _ = x_sample.shape
    y_s, ssm_s, conv_s, kv_s, logf_s = _trunk(
        x_sample.reshape(b_s * seq_s, d_model), state_conv[0], state_ssm[0], p, w,
        batch=b_s, seq=seq_s, tm=b_s * seq_s, attend=(cache_kv, cache_logf, page_table))

    return (y_p.reshape(b_p, seq_p, d_model), y_s.reshape(b_s, seq_s, d_model),
            ssm_p[None], conv_p[None],
            kv_p.reshape(b_p, seq_p, 2, n_heads, ATT_HEAD_DIM), logf_p.reshape(b_p, seq_p, n_heads),
            ssm_s[None], conv_s[None],
            kv_s.reshape(b_s, seq_s, 2, n_heads, ATT_HEAD_DIM), logf_s.reshape(b_s, seq_s, n_heads))
```
